```python
import jax, jax.numpy as jnp
from jax import lax
import numpy as np

D_MODEL = 2048
BATCH = 2
SEQ = 4096
DEPTH = 4
DEC_BATCH = 8
DEC_SEQ = 1
PAST_LEN = 16384
PAGE_SIZE = 128

N_HEADS = 16
HEAD_DIM = 128
KV_GROUPS = 4
GQA_R = N_HEADS // KV_GROUPS
ROPE_DIM = HEAD_DIM // 4
ROPE_THETA = 500000.0
SCALE = HEAD_DIM ** -0.5
CMP_LEN = 32
CMP_STRIDE = 16
CMP_HID = HEAD_DIM
SEL_BLOCK = 64
SEL_TOPK = 16
SEL_Q_BLOCK = 64
WINDOW = 512
Q_BLOCK = 128
POOL_GROUPS = 4
POOL_WINDOWS = (2, 4, 8, 16)
POOL_W = D_MODEL // 2
POOL_GW = POOL_W // POOL_GROUPS
POOL_BUF = max(POOL_WINDOWS) - 1
D_FF = 5632
N_EXPERTS = 8
TOP_K = 2
PLE_DIM = 256
RMS_EPS = 1e-6
Q_W = N_HEADS * HEAD_DIM
KV_W = KV_GROUPS * HEAD_DIM
N_GATES_NSA = 3 * N_HEADS
N_IN = POOL_W + Q_W + 6 * KV_W + N_GATES_NSA + 2 * D_MODEL
IN_SPLITS = tuple(int(s) for s in np.cumsum([POOL_W, Q_W, KV_W, KV_W, KV_W, KV_W, KV_W, KV_W, N_GATES_NSA, D_MODEL]))
N_DENSE = (DEPTH + 1) // 2
N_MOE = DEPTH // 2

kernel_name = 'pool_nsa_hybrid_decode_step'


def rmsnorm(x, g):
    xf = x.astype(jnp.float32)
    y = xf * lax.rsqrt(jnp.mean(xf * xf, axis=-1, keepdims=True) + RMS_EPS)
    return (y * g.astype(jnp.float32)).astype(x.dtype)


def rope(x, pos):
    half = ROPE_DIM // 2
    inv = jnp.power(jnp.float32(ROPE_THETA), -jnp.arange(half, dtype=jnp.float32) * (2.0 / ROPE_DIM))
    ang = pos.astype(jnp.float32)[:, None] * inv[None, :]
    cos = jnp.cos(ang)[None, :, None, :]
    sin = jnp.sin(ang)[None, :, None, :]
    xf = x.astype(jnp.float32)
    x1, x2, rest = xf[..., :half], xf[..., half:ROPE_DIM], xf[..., ROPE_DIM:]
    return jnp.concatenate([x1 * cos - x2 * sin, x2 * cos + x1 * sin, rest], axis=-1).astype(x.dtype)


def masked_softmax(s, mask):
    s = jnp.where(mask, s, -jnp.inf)
    m = jnp.max(s, axis=-1, keepdims=True)
    m = jnp.where(jnp.isfinite(m), m, 0.0)
    e = jnp.where(mask, jnp.exp(s - m), 0.0)
    d = jnp.sum(e, axis=-1, keepdims=True)
    return e / jnp.where(d > 0, d, 1.0)


def compress(k, w1, w2):
    B, L, G, Dh = k.shape
    r = CMP_LEN // CMP_STRIDE
    n = L // CMP_STRIDE - r + 1
    kb = k.reshape(B, L // CMP_STRIDE, CMP_STRIDE, G, Dh)
    w1r = w1.reshape(r, CMP_STRIDE, Dh, CMP_HID)
    pre = sum(jnp.einsum('bnjgd,jdh->bngh', kb[:, o:o + n], w1r[o]) for o in range(r))
    return jnp.einsum('bngh,hd->bngd', jax.nn.gelu(pre), w2)


def select_attn(q, qpos, idx, ks_b, vs_b):
    B, Tb, G, R, Dh = q.shape
    k = idx.shape[-1]
    flat = jnp.swapaxes(idx, 1, 2).reshape(B, G, Tb * k)
    bi = jnp.arange(B)[:, None, None]
    gi = jnp.arange(G)[None, :, None]
    kg = ks_b[bi, gi, flat].reshape(B, G, Tb, k * SEL_BLOCK, Dh)
    vg = vs_b[bi, gi, flat].reshape(B, G, Tb, k * SEL_BLOCK, Dh)
    s = jnp.einsum('btgrd,bgtmd->btgrm', q, kg).astype(jnp.float32) * SCALE
    kpos = (idx[..., None] * SEL_BLOCK + jnp.arange(SEL_BLOCK)).reshape(B, Tb, G, 1, k * SEL_BLOCK)
    p = masked_softmax(s, kpos <= qpos[None, :, None, None, None])
    return jnp.einsum('btgrm,bgtmd->btgrd', p.astype(vg.dtype), vg)


def nsa_global(qn, qr, qpos, kc, vc, ks, vs, kc_g, w_ck1, w_ck2, w_cv1, w_cv2):
    B, T, H, Dh = qn.shape
    L = kc.shape[1]
    Lp = -(-L // SEL_BLOCK) * SEL_BLOCK
    padw = ((0, 0), (0, Lp - L), (0, 0), (0, 0))
    kc, vc, ks, vs = [jnp.pad(a, padw) for a in (kc, vc, ks, vs)]
    kcmp = rmsnorm(compress(kc, w_ck1, w_ck2), kc_g)
    vcmp = compress(vc, w_cv1, w_cv2)
    n_cmp = kcmp.shape[1]
    qg = qn.reshape(B, T, KV_GROUPS, GQA_R, Dh)
    s = jnp.einsum('btgrd,bngd->btgrn', qg, kcmp).astype(jnp.float32) * SCALE
    cmp_end = jnp.arange(n_cmp) * CMP_STRIDE + (CMP_LEN - 1)
    mask_c = (cmp_end[None, :] <= qpos[:, None])[None, :, None, None, :]
    p_cmp = masked_softmax(s, mask_c)
    o_cmp = jnp.einsum('btgrn,bngd->btgrd', p_cmp.astype(vcmp.dtype), vcmp)
    n_sel = Lp // SEL_BLOCK
    imp_c = jnp.sum(p_cmp, axis=3)
    offs = np.arange(-(CMP_LEN // CMP_STRIDE - 1), SEL_BLOCK // CMP_STRIDE)
    cidx = (SEL_BLOCK // CMP_STRIDE) * np.arange(n_sel)[:, None] + offs[None, :]
    cvalid = (cidx >= 0) & (cidx < n_cmp)
    imp = jnp.sum(jnp.where(cvalid, jnp.take(imp_c, np.clip(cidx, 0, n_cmp - 1), axis=-1), 0.0), axis=-1)
    blk = jnp.arange(n_sel)[None, :]
    cur = (qpos // SEL_BLOCK)[:, None]
    forced = (blk == 0) | (blk == cur) | (blk == cur - 1)
    causal = blk * SEL_BLOCK <= qpos[:, None]
    score = jnp.where(forced[None, :, None, :], jnp.inf, jnp.where(causal[None, :, None, :], imp, -jnp.inf))
    _, idx = lax.top_k(score, min(SEL_TOPK, n_sel))

    def to_blocks(a):
        a = a.reshape(B, n_sel, SEL_BLOCK, KV_GROUPS, Dh)
        return jnp.transpose(a, (0, 3, 1, 2, 4)).reshape(B, KV_GROUPS, n_sel, SEL_BLOCK * Dh)

    ks_b, vs_b = to_blocks(ks), to_blocks(vs)
    qr_g = qr.reshape(B, T, KV_GROUPS, GQA_R, Dh)
    if T > SEL_Q_BLOCK and T % SEL_Q_BLOCK == 0:
        nb = T // SEL_Q_BLOCK
        qb = jnp.swapaxes(qr_g.reshape(B, nb, SEL_Q_BLOCK, KV_GROUPS, GQA_R, Dh), 0, 1)
        pb = qpos.reshape(nb, SEL_Q_BLOCK)
        ib = jnp.swapaxes(idx.reshape(B, nb, SEL_Q_BLOCK, KV_GROUPS, idx.shape[-1]), 0, 1)
        ob = lax.map(lambda a: select_attn(a[0], a[1], a[2], ks_b, vs_b), (qb, pb, ib))
        o_slc = jnp.swapaxes(ob, 0, 1).reshape(B, T, KV_GROUPS, GQA_R, Dh)
    else:
        o_slc = select_attn(qr_g, qpos, idx, ks_b, vs_b)
    return o_cmp, o_slc


def band_attn(q, k, v, qpos, kpos):
    s = jnp.einsum('bntgrd,bnlgd->bntgrl', q, k).astype(jnp.float32) * SCALE
    kp = kpos[:, None, :]
    qp = qpos[:, :, None]
    mask = ((kp <= qp) & (kp > qp - WINDOW) & (kp >= 0))[None, :, :, None, None, :]
    p = masked_softmax(s, mask)
    return jnp.einsum('bntgrl,bnlgd->bntgrd', p.astype(v.dtype), v)


def window_prompt(qr_g, k, v):
    B, S = k.shape[0], k.shape[1]
    nb = S // Q_BLOCK
    nw = WINDOW // Q_BLOCK

    def banded(a):
        ap = jnp.pad(a, ((0, 0), (WINDOW, 0), (0, 0), (0, 0))).reshape(B, nb + nw, Q_BLOCK, KV_GROUPS, HEAD_DIM)
        return jnp.concatenate([ap[:, o:o + nb] for o in range(nw + 1)], axis=2)

    qpos = jnp.arange(S).reshape(nb, Q_BLOCK)
    kpos = jnp.arange(nb)[:, None] * Q_BLOCK - WINDOW + jnp.arange((nw + 1) * Q_BLOCK)[None, :]
    o = band_attn(qr_g.reshape(B, nb, Q_BLOCK, KV_GROUPS, GQA_R, HEAD_DIM), banded(k), banded(v), qpos, kpos)
    return o.reshape(B, S, KV_GROUPS, GQA_R, HEAD_DIM)


def pool_mix(u_ext, n_prefix, pool_w, pool_scale):
    B, L, C = u_ext.shape
    T = L - n_prefix
    cs = jnp.concatenate([jnp.zeros((B, 1, C), jnp.float32), jnp.cumsum(u_ext.astype(jnp.float32), axis=1)], axis=1)
    i = jnp.arange(n_prefix, L)
    means = []
    for g, w in enumerate(POOL_WINDOWS):
        st = jnp.maximum(i + 1 - w, 0)
        cnt = (i + 1 - st).astype(jnp.float32)
        c = cs[:, :, g * POOL_GW:(g + 1) * POOL_GW]
        means.append((c[:, i + 1] - c[:, st]) / cnt[None, :, None])
    z = jnp.concatenate(means, axis=-1) - u_ext[:, n_prefix:].astype(jnp.float32)
    z = z.astype(u_ext.dtype).reshape(B, T, POOL_GROUPS, POOL_GW)
    y = jnp.einsum('btgc,gcd->btgd', z, pool_w).reshape(B, T, POOL_W)
    return y * pool_scale


def last_rows(a, n):
    if a.shape[1] < n:
        a = jnp.pad(a, [(0, 0), (n - a.shape[1], 0)] + [(0, 0)] * (a.ndim - 2))
    return a[:, a.shape[1] - n:]


def dense_ffn(h, wg, wu, wd):
    return (jax.nn.silu(h @ wg) * (h @ wu)) @ wd


def moe_ffn(h, wr, wg, wu, wd):
    logits = (h @ wr).astype(jnp.float32)
    top_v, top_i = lax.top_k(logits, TOP_K)
    wts = jax.nn.softmax(top_v, axis=-1)
    gate = jnp.sum(jax.nn.one_hot(top_i, N_EXPERTS, dtype=jnp.float32) * wts[..., None], axis=-2).astype(h.dtype)
    out = jnp.zeros_like(h)
    for e in range(N_EXPERTS):
        out = out + gate[..., e:e + 1] * dense_ffn(h, wg[e], wu[e], wd[e])
    return out


def gather_pages(pool_l, page_table):
    g = pool_l[page_table]
    return g.reshape(g.shape[0], g.shape[1] * g.shape[2], g.shape[3], g.shape[4])


def run_layer(x, p_i, pos, lw, ffn_fn, past, win_buf):
    B, T, _ = x.shape
    h = rmsnorm(x, lw['attn_norm'])
    u, q, kc, vc, ks, vs, kw, vw, g_nsa, g_a, g_b = jnp.split(h @ lw['w_in'], IN_SPLITS, axis=-1)
    heads = lambda a, n: a.reshape(B, T, n, HEAD_DIM)
    qn = rmsnorm(heads(q, N_HEADS), lw['q_norm'])
    qr = rope(qn, pos)
    kc, vc, vs, vw = heads(kc, KV_GROUPS), heads(vc, KV_GROUPS), heads(vs, KV_GROUPS), heads(vw, KV_GROUPS)
    ks = rope(rmsnorm(heads(ks, KV_GROUPS), lw['ks_norm']), pos)
    kw = rope(rmsnorm(heads(kw, KV_GROUPS), lw['kw_norm']), pos)
    qr_g = qr.reshape(B, T, KV_GROUPS, GQA_R, HEAD_DIM)
    if past is None:
        kc_all, vc_all, ks_all, vs_all = kc, vc, ks, vs
        o_win = window_prompt(qr_g, kw, vw)
        kw_ext, vw_ext, u_ext, n_pre = kw, vw, u, 0
    else:
        kc_all = jnp.concatenate([past['kc'], kc], axis=1)
        vc_all = jnp.concatenate([past['vc'], vc], axis=1)
        ks_all = jnp.concatenate([past['ks'], ks], axis=1)
        vs_all = jnp.concatenate([past['vs'], vs], axis=1)
        kw_ext = jnp.concatenate([past['kw'], kw], axis=1)
        vw_ext = jnp.concatenate([past['vw'], vw], axis=1)
        u_ext = jnp.concatenate([past['pool'], u], axis=1)
        n_pre = past['pool'].shape[1]
        n_buf = past['kw'].shape[1]
        kpos = pos[0] - n_buf + jnp.arange(n_buf + T)
        o_win = band_attn(qr_g[:, None], kw_ext[:, None], vw_ext[:, None], pos[None], kpos[None])[:, 0]
    o_cmp, o_slc = nsa_global(qn, qr, pos, kc_all, vc_all, ks_all, vs_all, lw['kc_norm'],
                              lw['w_cmp_k1'], lw['w_cmp_k2'], lw['w_cmp_v1'], lw['w_cmp_v2'])
    gn = jax.nn.sigmoid(g_nsa.astype(jnp.float32)).reshape(B, T, 3, KV_GROUPS, GQA_R, 1).astype(x.dtype)
    o_nsa = (gn[:, :, 0] * o_cmp + gn[:, :, 1] * o_slc + gn[:, :, 2] * o_win).reshape(B, T, Q_W)
    a_pool = pool_mix(u_ext, n_pre, lw['pool_w'], lw['pool_scale'])
    merged = jax.nn.sigmoid(g_a) * (a_pool @ lw['w_branch_pool']) + jax.nn.sigmoid(g_b) * (o_nsa @ lw['w_branch_nsa'])
    x = x + merged @ lw['w_out']
    x = x + ffn_fn(rmsnorm(x, lw['ffn_norm']))
    x = x + jax.nn.sigmoid(rmsnorm(x, lw['ple_norm']) @ lw['w_ple_gate']) * (p_i @ lw['w_ple_proj'])
    new = (kc, vc, ks, vs, last_rows(kw_ext, win_buf), last_rows(vw_ext, win_buf), last_rows(u_ext, POOL_BUF))
    return x, new


def setup_inputs(seed: int = 0) -> dict:
    key = jax.random.key(seed)
    keys = iter(jax.random.split(key, 48))
    f32 = jnp.float32

    def rn(shape, scale=1.0):
        return jax.random.normal(next(keys), shape, f32) * scale

    def gain(shape):
        return 1.0 + 0.02 * jax.random.normal(next(keys), shape, f32)

    n_pages = PAST_LEN // PAGE_SIZE
    n_used = DEC_BATCH * n_pages
    n_pool = n_used + n_used // 4
    win_buf = min(WINDOW, PAST_LEN)
    cshape = (DEPTH, n_pool, PAGE_SIZE, KV_GROUPS, HEAD_DIM)
    page_table = jax.random.permutation(next(keys), n_pool)[:n_used].reshape(DEC_BATCH, n_pages).astype(jnp.int32)
    return {
        'x_prompt': rn((BATCH, SEQ, D_MODEL)),
        'x_sample': rn((DEC_BATCH, DEC_SEQ, D_MODEL)),
        'cache_k_cmp': rn(cshape),
        'cache_v_cmp': rn(cshape),
        'cache_k_slc': rn(cshape),
        'cache_v_slc': rn(cshape),
        'state_k_win': rn((DEPTH, DEC_BATCH, win_buf, KV_GROUPS, HEAD_DIM)),
        'state_v_win': rn((DEPTH, DEC_BATCH, win_buf, KV_GROUPS, HEAD_DIM)),
        'state_pool': rn((DEPTH, DEC_BATCH, POOL_BUF, POOL_W)),
        'page_table': page_table,
        'p_prompt': rn((DEPTH, BATCH, SEQ, PLE_DIM)),
        'p_sample': rn((DEPTH, DEC_BATCH, DEC_SEQ, PLE_DIM)),
        'attn_norm': gain((DEPTH, D_MODEL)),
        'w_in': rn((DEPTH, D_MODEL, N_IN), D_MODEL ** -0.5),
        'q_norm': gain((DEPTH, HEAD_DIM)),
        'kc_norm': gain((DEPTH, HEAD_DIM)),
        'ks_norm': gain((DEPTH, HEAD_DIM)),
        'kw_norm': gain((DEPTH, HEAD_DIM)),
        'w_cmp_k1': rn((DEPTH, CMP_LEN * HEAD_DIM, CMP_HID), (CMP_LEN * HEAD_DIM) ** -0.5),
        'w_cmp_k2': rn((DEPTH, CMP_HID, HEAD_DIM), CMP_HID ** -0.5),
        'w_cmp_v1': rn((DEPTH, CMP_LEN * HEAD_DIM, CMP_HID), (CMP_LEN * HEAD_DIM) ** -0.5),
        'w_cmp_v2': rn((DEPTH, CMP_HID, HEAD_DIM), CMP_HID ** -0.5),
        'pool_w': rn((DEPTH, POOL_GROUPS, POOL_GW, POOL_GW), POOL_GW ** -0.5),
        'pool_scale': gain((DEPTH, POOL_W)),
        'w_branch_pool': rn((DEPTH, POOL_W, D_MODEL), POOL_W ** -0.5),
        'w_branch_nsa': rn((DEPTH, Q_W, D_MODEL), Q_W ** -0.5),
        'w_out': rn((DEPTH, D_MODEL, D_MODEL), D_MODEL ** -0.5),
        'ffn_norm': gain((DEPTH, D_MODEL)),
        'w_gate_d': rn((N_DENSE, D_MODEL, D_FF), D_MODEL ** -0.5),
        'w_up_d': rn((N_DENSE, D_MODEL, D_FF), D_MODEL ** -0.5),
        'w_down_d': rn((N_DENSE, D_FF, D_MODEL), D_FF ** -0.5),
        'w_router': rn((N_MOE, D_MODEL, N_EXPERTS), D_MODEL ** -0.5),
        'w_gate_e': rn((N_MOE, N_EXPERTS, D_MODEL, D_FF), D_MODEL ** -0.5),
        'w_up_e': rn((N_MOE, N_EXPERTS, D_MODEL, D_FF), D_MODEL ** -0.5),
        'w_down_e': rn((N_MOE, N_EXPERTS, D_FF, D_MODEL), D_FF ** -0.5),
        'ple_norm': gain((DEPTH, D_MODEL)),
        'w_ple_gate': rn((DEPTH, D_MODEL, D_MODEL), D_MODEL ** -0.5),
        'w_ple_proj': rn((DEPTH, PLE_DIM, D_MODEL), PLE_DIM ** -0.5),
    }


def reference(x_prompt, x_sample, cache_k_cmp, cache_v_cmp, cache_k_slc, cache_v_slc, state_k_win, state_v_win,
              state_pool, page_table, p_prompt, p_sample, attn_norm, w_in, q_norm, kc_norm, ks_norm, kw_norm,
              w_cmp_k1, w_cmp_k2, w_cmp_v1, w_cmp_v2, pool_w, pool_scale, w_branch_pool, w_branch_nsa, w_out,
              ffn_norm, w_gate_d, w_up_d, w_down_d, w_router, w_gate_e, w_up_e, w_down_e, ple_norm, w_ple_gate,
              w_ple_proj):
    win_buf = state_k_win.shape[2]
    past_len = page_table.shape[1] * cache_k_cmp.shape[2]
    pos_p = jnp.arange(x_prompt.shape[1])
    pos_s = past_len + jnp.arange(x_sample.shape[1])
    xp, xs = x_prompt, x_sample
    states_p, states_s = [], []
    for li in range(DEPTH):
        lw = dict(attn_norm=attn_norm[li], w_in=w_in[li], q_norm=q_norm[li], kc_norm=kc_norm[li],
                  ks_norm=ks_norm[li], kw_norm=kw_norm[li], w_cmp_k1=w_cmp_k1[li], w_cmp_k2=w_cmp_k2[li],
                  w_cmp_v1=w_cmp_v1[li], w_cmp_v2=w_cmp_v2[li], pool_w=pool_w[li], pool_scale=pool_scale[li],
                  w_branch_pool=w_branch_pool[li], w_branch_nsa=w_branch_nsa[li], w_out=w_out[li],
                  ffn_norm=ffn_norm[li], ple_norm=ple_norm[li], w_ple_gate=w_ple_gate[li], w_ple_proj=w_ple_proj[li])
        j = li // 2
        if li % 2 == 0:
            ffn_fn = lambda h, j=j: dense_ffn(h, w_gate_d[j], w_up_d[j], w_down_d[j])
        else:
            ffn_fn = lambda h, j=j: moe_ffn(h, w_router[j], w_gate_e[j], w_up_e[j], w_down_e[j])
        past = dict(kc=gather_pages(cache_k_cmp[li], page_table), vc=gather_pages(cache_v_cmp[li], page_table),
                    ks=gather_pages(cache_k_slc[li], page_table), vs=gather_pages(cache_v_slc[li], page_table),
                    kw=state_k_win[li], vw=state_v_win[li], pool=state_pool[li])
        xp, st_p = run_layer(xp, p_prompt[li], pos_p, lw, ffn_fn, None, win_buf)
        xs, st_s = run_layer(xs, p_sample[li], pos_s, lw, ffn_fn, past, win_buf)
        states_p.append(st_p)
        states_s.append(st_s)
    k_cmp_p, v_cmp_p, k_slc_p, v_slc_p, k_win_p, v_win_p, pool_p = (jnp.stack(a) for a in zip(*states_p))
    k_cmp_s, v_cmp_s, k_slc_s, v_slc_s, k_win_s, v_win_s, pool_s = (jnp.stack(a) for a in zip(*states_s))
    return (xp, xs, k_cmp_p, v_cmp_p, k_slc_p, v_slc_p, k_win_p, v_win_p, pool_p,
            k_cmp_s, v_cmp_s, k_slc_s, v_slc_s, k_win_s, v_win_s, pool_s)
```

```python
import functools

import numpy as np
import jax
import jax.numpy as jnp
from jax import lax
from jax.experimental import pallas as pl
from jax.experimental.pallas import tpu as pltpu

F32 = jnp.float32
MXU = jnp.bfloat16

D_MODEL = 2048
N_HEADS = 16
HEAD_DIM = 128
KV_GROUPS = 4
GQA_R = N_HEADS // KV_GROUPS
ROPE_DIM = HEAD_DIM // 4
ROPE_THETA = 500000.0
SCALE = HEAD_DIM ** -0.5
CMP_LEN = 32
CMP_STRIDE = 16
SEL_BLOCK = 64
SEL_TOPK = 16
WINDOW = 512
POOL_GROUPS = 4
POOL_WINDOWS = (2, 4, 8, 16)
POOL_W = D_MODEL // 2
POOL_GW = POOL_W // POOL_GROUPS
POOL_BUF = max(POOL_WINDOWS) - 1
N_EXPERTS = 8
PLE_DIM = 256
RMS_EPS = 1e-6
Q_W = N_HEADS * HEAD_DIM
KV_W = KV_GROUPS * HEAD_DIM
N_GATES_NSA = 3 * N_HEADS
PAGE = 128

LANES = 128
VMEM_LIMIT_BYTES = 56 * 1024 * 1024
NEG_BIG = -30000.0
NEG_INF = -1e30

OFF_Q, OFF_U, OFF_KC, OFF_VC, OFF_KS, OFF_VS, OFF_KW, OFF_VW = 0, 2048, 3072, 3584, 4096, 4608, 5120, 5632
OFF_GA, OFF_GB, OFF_GN = 6144, 8192, 10240
N_PACK = 10368
PACK_TN = 1152


def _cp(sem):
    return pltpu.CompilerParams(dimension_semantics=sem, vmem_limit_bytes=VMEM_LIMIT_BYTES)


def _rms(x, g):
    ms = jnp.mean(x * x, axis=-1, keepdims=True)
    return x * lax.rsqrt(ms + RMS_EPS) * g


def _dot(a, b):
    return jnp.dot(a, b, preferred_element_type=F32)


def _dot_nt(a, b):
    return lax.dot_general(a, b, (((1,), (1,)), ((), ())), preferred_element_type=F32)


def _dot_f32_by_01(p, sel):
    if MXU == F32:
        return _dot(p, sel)
    hi = p.astype(MXU)
    r1 = p - hi.astype(F32)
    mid = r1.astype(MXU)
    lo = (r1 - mid.astype(F32)).astype(MXU)
    return _dot(hi, sel) + _dot(mid, sel) + _dot(lo, sel)


def _softmax_rows(s, mask):
    s = jnp.where(mask, s, NEG_INF)
    m = jnp.max(s, axis=-1, keepdims=True)
    e = jnp.where(mask, jnp.exp(s - m), 0.0)
    d = jnp.sum(e, axis=-1, keepdims=True)
    return e / jnp.where(d > 0, d, 1.0)


def _norm_mm_body(x_ref, g_ref, w_ref, o_ref, h_ref):
    @pl.when(pl.program_id(1) == 0)
    def _():
        h_ref[...] = _rms(x_ref[...], g_ref[...]).astype(MXU)

    o_ref[...] = _dot(h_ref[...], w_ref[...])


def norm_mm(x, g, w, tm, tn):
    M, K = x.shape
    N = w.shape[1]
    return pl.pallas_call(
        _norm_mm_body, grid=(M // tm, N // tn),
        in_specs=[pl.BlockSpec((tm, K), lambda i, j: (i, 0)),
                  pl.BlockSpec((1, K), lambda i, j: (0, 0)),
                  pl.BlockSpec((K, tn), lambda i, j: (0, j))],
        out_specs=pl.BlockSpec((tm, tn), lambda i, j: (i, j)),
        out_shape=jax.ShapeDtypeStruct((M, N), F32),
        scratch_shapes=[pltpu.VMEM((tm, K), MXU)],
        compiler_params=_cp(("parallel", "arbitrary")), name="norm_mm")(x, g.reshape(1, K), w)


def _mm_res_body(a_ref, w_ref, r_ref, o_ref):
    o_ref[...] = r_ref[...] + _dot(a_ref[...], w_ref[...])


def _mm_res_scaled_body(a_ref, w_ref, r_ref, s_ref, o_ref, *, col):
    lane = lax.broadcasted_iota(jnp.int32, s_ref.shape, 1)
    sc = jnp.sum(jnp.where(lane == col, s_ref[...], 0.0), axis=1, keepdims=True)
    o_ref[...] = r_ref[...] + sc * _dot(a_ref[...], w_ref[...])


def mm_res(a, w, res, tm, tn, scale=None, col=0, wsel=None):
    M, K = a.shape
    N = res.shape[1]
    if wsel is None:
        w_spec = pl.BlockSpec((K, tn), lambda i, j: (0, j))
    else:
        w_spec = pl.BlockSpec((None, K, tn), lambda i, j: (wsel, 0, j))
    in_specs = [pl.BlockSpec((tm, K), lambda i, j: (i, 0)), w_spec,
                pl.BlockSpec((tm, tn), lambda i, j: (i, j))]
    args = [a, w, res]
    body = _mm_res_body
    if scale is not None:
        in_specs.append(pl.BlockSpec((tm, LANES), lambda i, j: (i, 0)))
        args.append(scale)
        body = functools.partial(_mm_res_scaled_body, col=col)
    return pl.pallas_call(
        body, grid=(M // tm, N // tn), in_specs=in_specs,
        out_specs=pl.BlockSpec((tm, tn), lambda i, j: (i, j)),
        out_shape=jax.ShapeDtypeStruct((M, N), F32),
        compiler_params=_cp(("parallel", "arbitrary")), name="mm_res")(*args)


def _ffn_gu_body(x_ref, g_ref, wg_ref, wu_ref, o_ref, h_ref):
    @pl.when(pl.program_id(1) == 0)
    def _():
        h_ref[...] = _rms(x_ref[...], g_ref[...]).astype(MXU)

    h = h_ref[...]
    a = _dot(h, wg_ref[...])
    b = _dot(h, wu_ref[...])
    o_ref[...] = (a * jax.nn.sigmoid(a) * b).astype(o_ref.dtype)


def ffn_gu(x, g, wg, wu, tm, tf, wsel=None):
    M, K = x.shape
    dff = wg.shape[-1]
    if wsel is None:
        w_spec = pl.BlockSpec((K, tf), lambda i, j: (0, j))
    else:
        w_spec = pl.BlockSpec((None, K, tf), lambda i, j: (wsel, 0, j))
    return pl.pallas_call(
        _ffn_gu_body, grid=(M // tm, dff // tf),
        in_specs=[pl.BlockSpec((tm, K), lambda i, j: (i, 0)),
                  pl.BlockSpec((1, K), lambda i, j: (0, 0)), w_spec, w_spec],
        out_specs=pl.BlockSpec((tm, tf), lambda i, j: (i, j)),
        out_shape=jax.ShapeDtypeStruct((M, dff), MXU),
        scratch_shapes=[pltpu.VMEM((tm, K), MXU)],
        compiler_params=_cp(("parallel", "arbitrary")), name="ffn_gu")(x, g.reshape(1, K), wg, wu)


def _merge_body(ap_ref, on_ref, wp_ref, wn_ref, ga_ref, gb_ref, o_ref):
    yp = _dot(ap_ref[...], wp_ref[...])
    yn = _dot(on_ref[...], wn_ref[...])
    o_ref[...] = (jax.nn.sigmoid(ga_ref[...]) * yp + jax.nn.sigmoid(gb_ref[...]) * yn).astype(o_ref.dtype)


def merge(a_pool, o_nsa, wbp, wbn, y, tm, tn):
    M = a_pool.shape[0]
    ca, cb = OFF_GA // tn, OFF_GB // tn
    return pl.pallas_call(
        _merge_body, grid=(M // tm, D_MODEL // tn),
        in_specs=[pl.BlockSpec((tm, POOL_W), lambda i, j: (i, 0)),
                  pl.BlockSpec((tm, Q_W), lambda i, j: (i, 0)),
                  pl.BlockSpec((POOL_W, tn), lambda i, j: (0, j)),
                  pl.BlockSpec((Q_W, tn), lambda i, j: (0, j)),
                  pl.BlockSpec((tm, tn), lambda i, j: (i, ca + j)),
                  pl.BlockSpec((tm, tn), lambda i, j: (i, cb + j))],
        out_specs=pl.BlockSpec((tm, tn), lambda i, j: (i, j)),
        out_shape=jax.ShapeDtypeStruct((M, D_MODEL), MXU),
        compiler_params=_cp(("parallel", "arbitrary")), name="merge")(a_pool, o_nsa, wbp, wbn, y, y)


def _ple_body(x_ref, xt_ref, g_ref, wg_ref, p_ref, wp_ref, o_ref, h_ref):
    @pl.when(pl.program_id(1) == 0)
    def _():
        h_ref[...] = _rms(x_ref[...], g_ref[...]).astype(MXU)

    gate = jax.nn.sigmoid(_dot(h_ref[...], wg_ref[...]))
    proj = _dot(p_ref[...].astype(MXU), wp_ref[...])
    o_ref[...] = xt_ref[...] + gate * proj


def ple(x, g, wg, p, wp, tm, tn):
    M, K = x.shape
    return pl.pallas_call(
        _ple_body, grid=(M // tm, K // tn),
        in_specs=[pl.BlockSpec((tm, K), lambda i, j: (i, 0)),
                  pl.BlockSpec((tm, tn), lambda i, j: (i, j)),
                  pl.BlockSpec((1, K), lambda i, j: (0, 0)),
                  pl.BlockSpec((K, tn), lambda i, j: (0, j)),
                  pl.BlockSpec((tm, PLE_DIM), lambda i, j: (i, 0)),
                  pl.BlockSpec((PLE_DIM, tn), lambda i, j: (0, j))],
        out_specs=pl.BlockSpec((tm, tn), lambda i, j: (i, j)),
        out_shape=jax.ShapeDtypeStruct((M, K), F32),
        scratch_shapes=[pltpu.VMEM((tm, K), MXU)],
        compiler_params=_cp(("parallel", "arbitrary")), name="ple")(x, x, g.reshape(1, K), wg, p, wp)


def _router_body(x_ref, g_ref, w_ref, o_ref):
    h = _rms(x_ref[...], g_ref[...])
    logits = jnp.dot(h, w_ref[...], preferred_element_type=F32, precision=lax.Precision.HIGHEST)
    lane = lax.broadcasted_iota(jnp.int32, logits.shape, 1)
    logits = jnp.where(lane < N_EXPERTS, logits, NEG_INF)
    m1 = jnp.max(logits, axis=1, keepdims=True)
    i1 = jnp.min(jnp.where(logits == m1, lane, LANES), axis=1, keepdims=True)
    rest = jnp.where(lane == i1, NEG_INF, logits)
    m2 = jnp.max(rest, axis=1, keepdims=True)
    i2 = jnp.min(jnp.where(rest == m2, lane, LANES), axis=1, keepdims=True)
    e2 = jnp.exp(m2 - m1)
    w1 = 1.0 / (1.0 + e2)
    w2 = e2 / (1.0 + e2)
    o_ref[...] = jnp.where(lane == i1, w1, 0.0) + jnp.where(lane == i2, w2, 0.0)


def router(x, g, wr_pad, tm):
    M, K = x.shape
    return pl.pallas_call(
        _router_body, grid=(M // tm,),
        in_specs=[pl.BlockSpec((tm, K), lambda i: (i, 0)),
                  pl.BlockSpec((1, K), lambda i: (0, 0)),
                  pl.BlockSpec((K, LANES), lambda i: (0, 0))],
        out_specs=pl.BlockSpec((tm, LANES), lambda i: (i, 0)),
        out_shape=jax.ShapeDtypeStruct((M, LANES), F32),
        compiler_params=_cp(("parallel",)), name="router")(x, g.reshape(1, K), wr_pad)


def _rope(xh, c, s1, s2):
    return xh * c + pltpu.roll(xh, HEAD_DIM - ROPE_DIM // 2, 1) * s1 + pltpu.roll(xh, ROPE_DIM // 2, 1) * s2


def _qk_post_body(q_ref, ks_ref, vs_ref, kw_ref, vw_ref, c_ref, s1_ref, s2_ref, e_ref, gq_ref, gs_ref, gw_ref,
                  qn_ref, qr_ref, ksr_ref, kwr_ref, ksa_ref, kwb_ref, vsb_ref, vwb_ref):
    c, s1, s2 = c_ref[...], s1_ref[...], s2_ref[...]
    for h in range(N_HEADS):
        sl = slice(h * HEAD_DIM, (h + 1) * HEAD_DIM)
        qn = _rms(q_ref[:, sl], gq_ref[...])
        qn_ref[:, sl] = (qn * SCALE).astype(qn_ref.dtype)
        qr_ref[:, sl] = (_rope(qn, c, s1, s2) * SCALE).astype(qr_ref.dtype)
    for g in range(KV_GROUPS):
        sl = slice(g * HEAD_DIM, (g + 1) * HEAD_DIM)
        ks = _rope(_rms(ks_ref[:, sl], gs_ref[...]), c, s1, s2)
        kw = _rope(_rms(kw_ref[:, sl], gw_ref[...]), c, s1, s2)
        ksr_ref[:, sl] = ks
        kwr_ref[:, sl] = kw
        ksa_ref[:, 2 * g * HEAD_DIM:(2 * g + 1) * HEAD_DIM] = ks.astype(ksa_ref.dtype)
        ksa_ref[:, (2 * g + 1) * HEAD_DIM:(2 * g + 2) * HEAD_DIM] = e_ref[...]
        kwb_ref[:, sl] = kw.astype(kwb_ref.dtype)
    vsb_ref[...] = vs_ref[...].astype(vsb_ref.dtype)
    vwb_ref[...] = vw_ref[...].astype(vwb_ref.dtype)


def qk_post(y, tabs, etab, gq, gs, gw, tm):
    M = y.shape[0]
    ntb = tabs[0].shape[0] // tm
    kvb = lambda off: pl.BlockSpec((tm, KV_W), lambda i, off=off: (i, off // KV_W))
    tab = pl.BlockSpec((tm, HEAD_DIM), lambda i: (i % ntb, 0))
    gain = pl.BlockSpec((1, HEAD_DIM), lambda i: (0, 0))
    row = lambda w: pl.BlockSpec((tm, w), lambda i: (i, 0))
    sds = jax.ShapeDtypeStruct
    return pl.pallas_call(
        _qk_post_body, grid=(M // tm,),
        in_specs=[pl.BlockSpec((tm, Q_W), lambda i: (i, OFF_Q // Q_W)), kvb(OFF_KS), kvb(OFF_VS), kvb(OFF_KW),
                  kvb(OFF_VW), tab, tab, tab, tab, gain, gain, gain],
        out_specs=[row(Q_W), row(Q_W), row(KV_W), row(KV_W), row(2 * KV_W), row(KV_W), row(KV_W), row(KV_W)],
        out_shape=[sds((M, Q_W), MXU), sds((M, Q_W), MXU), sds((M, KV_W), F32), sds((M, KV_W), F32),
                   sds((M, 2 * KV_W), MXU), sds((M, KV_W), MXU), sds((M, KV_W), MXU), sds((M, KV_W), MXU)],
        compiler_params=_cp(("parallel",)), name="qk_post")(
            y, y, y, y, y, tabs[0], tabs[1], tabs[2], etab,
            gq.reshape(1, HEAD_DIM), gs.reshape(1, HEAD_DIM), gw.reshape(1, HEAD_DIM))


def _rope_tables(pos):
    half = ROPE_DIM // 2
    n = pos.shape[0]
    inv = jnp.power(jnp.float32(ROPE_THETA), -jnp.arange(half, dtype=F32) * (2.0 / ROPE_DIM))
    ang = pos.astype(F32)[:, None] * inv[None, :]
    cos, sin = jnp.cos(ang), jnp.sin(ang)
    z = lambda w: jnp.zeros((n, w), F32)
    c = jnp.concatenate([cos, cos, jnp.ones((n, HEAD_DIM - ROPE_DIM), F32)], axis=1)
    s1 = jnp.concatenate([-sin, z(HEAD_DIM - half)], axis=1)
    s2 = jnp.concatenate([z(half), sin, z(HEAD_DIM - ROPE_DIM)], axis=1)
    etab = (pos[:, None] // SEL_BLOCK == jnp.arange(HEAD_DIM)[None, :]).astype(MXU)
    return (c, s1, s2), etab


def _pool_body(u_ref, halo_ref, w_ref, sc_ref, o_ref, *, tp):
    i = pl.program_id(1)
    u = u_ref[...]
    first = (lax.broadcasted_iota(jnp.int32, halo_ref.shape, 0) * 0 + i) == 0
    prev = jnp.where(first, 0.0, halo_ref[...])
    ext = jnp.concatenate([prev, u], axis=0)
    a2 = ext[1:] + ext[:-1]
    a4 = a2[2:, POOL_GW:] + a2[:-2, POOL_GW:]
    a8 = a4[4:, POOL_GW:] + a4[:-4, POOL_GW:]
    a16 = a8[8:, POOL_GW:] + a8[:-8, POOL_GW:]
    sums = (a2[15:, :POOL_GW], a4[13:, :POOL_GW], a8[9:, :POOL_GW], a16[1:])
    t = (i * tp + lax.broadcasted_iota(jnp.int32, (tp, 1), 0) + 1).astype(F32)
    for g, w in enumerate(POOL_WINDOWS):
        sl = slice(g * POOL_GW, (g + 1) * POOL_GW)
        cnt = jnp.minimum(t, float(w))
        z = (sums[g] / cnt - u[:, sl]).astype(MXU)
        o_ref[:, sl] = (_dot(z, w_ref[g]) * sc_ref[:, sl]).astype(o_ref.dtype)


def pool_mix(u3, col, pool_w, pool_scale, tp):
    B, T, _ = u3.shape
    hb = tp // 16
    out = pl.pallas_call(
        functools.partial(_pool_body, tp=tp), grid=(B, T // tp),
        in_specs=[pl.BlockSpec((None, tp, POOL_W), lambda b, i: (b, i, col)),
                  pl.BlockSpec((None, 16, POOL_W), lambda b, i: (b, jnp.maximum(i * hb - 1, 0), col)),
                  pl.BlockSpec((POOL_GROUPS, POOL_GW, POOL_GW), lambda b, i: (0, 0, 0)),
                  pl.BlockSpec((1, POOL_W), lambda b, i: (0, 0))],
        out_specs=pl.BlockSpec((None, tp, POOL_W), lambda b, i: (b, i, 0)),
        out_shape=jax.ShapeDtypeStruct((B, T, POOL_W), MXU),
        compiler_params=_cp(("parallel", "arbitrary")), name="pool_mix")(
            u3, u3, pool_w, pool_scale.reshape(1, POOL_W))
    return out.reshape(B * T, POOL_W)


CMP_PAGES = 32


def _cmp_proj_body(pt_ref, *refs):
    del pt_ref
    pages, w_ref, o_ref = refs[:-2], refs[-2], refs[-1]
    per_page = PAGE // CMP_STRIDE
    cols = []
    for j in range(CMP_STRIDE):
        rows = [pg[pl.ds(j, per_page, stride=CMP_STRIDE), :] for pg in pages]
        cols.append(jnp.concatenate(rows, axis=0).astype(MXU))
    lhs = jnp.concatenate(cols, axis=1)
    o_ref[...] = _dot(lhs, w_ref[...])


def cmp_proj(pages3, col, page_ids, w1s):
    B, n_pg = page_ids.shape
    per_page = PAGE // CMP_STRIDE
    pps = min(CMP_PAGES, n_pg)
    steps = n_pg // pps
    in_specs = [pl.BlockSpec((None, PAGE, HEAD_DIM),
                             lambda b, s, g, pt, k=k: (pt[b * n_pg + s * pps + k], 0, col * KV_GROUPS + g))
                for k in range(pps)]
    in_specs.append(pl.BlockSpec((CMP_STRIDE * HEAD_DIM, 2 * HEAD_DIM), lambda b, s, g, pt: (0, 0)))
    rows = pps * per_page
    return pl.pallas_call(
        _cmp_proj_body,
        grid_spec=pltpu.PrefetchScalarGridSpec(
            num_scalar_prefetch=1, grid=(B, steps, KV_GROUPS), in_specs=in_specs,
            out_specs=pl.BlockSpec((None, rows, 2 * HEAD_DIM), lambda b, s, g, pt: (b, s, g))),
        out_shape=jax.ShapeDtypeStruct((B, n_pg * per_page, KV_GROUPS * 2 * HEAD_DIM), F32),
        compiler_params=_cp(("parallel", "arbitrary", "arbitrary")), name="cmp_proj")(
            page_ids.reshape(-1), *([pages3] * pps), w1s)


def _cmp_mlp_body(pk_ref, pv_ref, w2k_ref, w2v_ref, g_ref, ko_ref, vo_ref):
    def one(p_ref, w2_ref, g):
        p0 = p_ref[:, 2 * g * HEAD_DIM:(2 * g + 1) * HEAD_DIM]
        p1 = p_ref[:, (2 * g + 1) * HEAD_DIM:(2 * g + 2) * HEAD_DIM]
        pre = p0 + jnp.concatenate([p1[1:], jnp.zeros((1, HEAD_DIM), F32)], axis=0)
        return _dot(jax.nn.gelu(pre).astype(MXU), w2_ref[...])

    for g in range(KV_GROUPS):
        ko_ref[g] = _rms(one(pk_ref, w2k_ref, g), g_ref[...]).astype(ko_ref.dtype)
        vo_ref[g] = one(pv_ref, w2v_ref, g).astype(vo_ref.dtype)


def cmp_mlp(pk, pv, w2k, w2v, kc_g):
    B, n, w = pk.shape
    p_spec = pl.BlockSpec((None, n, w), lambda b: (b, 0, 0))
    w_spec = pl.BlockSpec((HEAD_DIM, HEAD_DIM), lambda b: (0, 0))
    o_spec = pl.BlockSpec((None, KV_GROUPS, n, HEAD_DIM), lambda b: (b, 0, 0, 0))
    sd = jax.ShapeDtypeStruct((B, KV_GROUPS, n, HEAD_DIM), MXU)
    return pl.pallas_call(
        _cmp_mlp_body, grid=(B,),
        in_specs=[p_spec, p_spec, w_spec, w_spec, pl.BlockSpec((1, HEAD_DIM), lambda b: (0, 0))],
        out_specs=[o_spec, o_spec], out_shape=[sd, sd],
        compiler_params=_cp(("parallel",)), name="cmp_mlp")(pk, pv, w2k, w2v, kc_g.reshape(1, HEAD_DIM))


def _sel_matrix(n_cmp_rows, n_lanes):
    per = SEL_BLOCK // CMP_STRIDE
    c = np.arange(n_cmp_rows)[:, None]
    j = np.arange(n_lanes)[None, :]
    lo = per * j - (CMP_LEN // CMP_STRIDE - 1)
    return jnp.asarray((c >= lo) & (c <= per * j + per - 1), dtype=MXU)


def _stack_heads(x):
    return jnp.concatenate([x[:, r * HEAD_DIM:(r + 1) * HEAD_DIM] for r in range(GQA_R)], axis=0)


def _cmp_attn_body(q_ref, k_ref, v_ref, sel_ref, o_ref, sn_ref, *, tq, n_sel):
    i = pl.program_id(2)
    qs = _stack_heads(q_ref[...])
    s = _dot_nt(qs, k_ref[...])
    nc = s.shape[1]
    t_rows = i * tq + (lax.broadcasted_iota(jnp.int32, s.shape, 0) & (tq - 1))
    cmp_end = lax.broadcasted_iota(jnp.int32, s.shape, 1) * CMP_STRIDE + (CMP_LEN - 1)
    p = _softmax_rows(s, cmp_end <= t_rows)
    o = _dot(p.astype(MXU), v_ref[...])
    for r in range(GQA_R):
        o_ref[:, r * HEAD_DIM:(r + 1) * HEAD_DIM] = o[r * tq:(r + 1) * tq]
    p_sum = p[0:tq] + p[tq:2 * tq] + p[2 * tq:3 * tq] + p[3 * tq:4 * tq]
    imp = _dot_f32_by_01(p_sum, sel_ref[...])
    t = i * tq + lax.broadcasted_iota(jnp.int32, imp.shape, 0)
    blk = lax.broadcasted_iota(jnp.int32, imp.shape, 1)
    cur = t // SEL_BLOCK
    forced = (blk == 0) | (blk == cur) | (blk == cur - 1)
    score = jnp.where(forced, jnp.inf, jnp.where(blk * SEL_BLOCK <= t, imp, -jnp.inf))
    st = jnp.transpose(score)[:n_sel]
    nb = n_sel // 8
    tiles = [st[8 * a:8 * a + 8] for a in range(nb)]
    ranks = [jnp.zeros((8, tq), F32) for _ in range(nb)]
    sub = lax.broadcasted_iota(jnp.int32, (8, tq), 0)
    for b in range(n_sel):
        row = jnp.broadcast_to(st[b:b + 1], (8, tq))
        for a in range(nb):
            ge = jnp.where(row >= tiles[a], 1.0, 0.0)
            gt = jnp.where(row > tiles[a], 1.0, 0.0)
            if 8 * a > b:
                beats = ge
            elif 8 * a + 7 < b:
                beats = gt
            else:
                beats = jnp.where(sub + 8 * a > b, ge, gt)
            ranks[a] = ranks[a] + beats
    neg = [jnp.where(r < float(SEL_TOPK), 0.0, NEG_BIG) for r in ranks]
    neg_t = jnp.concatenate(neg + [jnp.zeros((LANES - n_sel, tq), F32)], axis=0)
    sn_ref[...] = jnp.transpose(neg_t).astype(sn_ref.dtype)


def cmp_attn(qn, kcmp, vcmp, T, tq):
    M = qn.shape[0]
    B = M // T
    nc = kcmp.shape[2]
    n_sel = T // SEL_BLOCK
    nt = T // tq
    sel = _sel_matrix(nc, LANES)
    kv_spec = pl.BlockSpec((None, None, nc, HEAD_DIM), lambda b, g, i: (b, g, 0, 0))
    return pl.pallas_call(
        functools.partial(_cmp_attn_body, tq=tq, n_sel=n_sel), grid=(B, KV_GROUPS, nt),
        in_specs=[pl.BlockSpec((tq, GQA_R * HEAD_DIM), lambda b, g, i: (b * nt + i, g)), kv_spec, kv_spec,
                  pl.BlockSpec((nc, LANES), lambda b, g, i: (0, 0))],
        out_specs=[pl.BlockSpec((tq, GQA_R * HEAD_DIM), lambda b, g, i: (b * nt + i, g)),
                   pl.BlockSpec((None, tq, LANES), lambda b, g, i: (g, b * nt + i, 0))],
        out_shape=[jax.ShapeDtypeStruct((M, Q_W), F32), jax.ShapeDtypeStruct((KV_GROUPS, M, LANES), MXU)],
        compiler_params=_cp(("parallel", "parallel", "arbitrary")), name="cmp_attn")(qn, kcmp, vcmp, sel)


def _slc_body(q_ref, sn_ref, k_ref, v_ref, o_ref, qa_ref, m_ref, l_ref, acc_ref, *, tq, tk):
    i, j = pl.program_id(2), pl.program_id(3)
    last = (i * tq + tq - 1) // tk

    @pl.when(j == 0)
    def _():
        sn = sn_ref[...]
        for r in range(GQA_R):
            qa_ref[r * tq:(r + 1) * tq, :HEAD_DIM] = q_ref[:, r * HEAD_DIM:(r + 1) * HEAD_DIM]
            qa_ref[r * tq:(r + 1) * tq, HEAD_DIM:] = sn
        m_ref[...] = jnp.full(m_ref.shape, NEG_INF, F32)
        l_ref[...] = jnp.zeros(l_ref.shape, F32)
        acc_ref[...] = jnp.zeros(acc_ref.shape, F32)

    @pl.when(j <= last)
    def _():
        s = _dot_nt(qa_ref[...], k_ref[...])
        t_rows = i * tq + (lax.broadcasted_iota(jnp.int32, s.shape, 0) & (tq - 1))
        kpos = j * tk + lax.broadcasted_iota(jnp.int32, s.shape, 1)
        s = jnp.where(kpos <= t_rows, s, NEG_INF)
        m_old = m_ref[...]
        m_new = jnp.maximum(m_old, jnp.max(s, axis=1, keepdims=True))
        alpha = jnp.exp(m_old - m_new)
        p = jnp.exp(s - m_new)
        l_ref[...] = alpha * l_ref[...] + jnp.sum(p, axis=1, keepdims=True)
        acc_ref[...] = alpha * acc_ref[...] + _dot(p.astype(MXU), v_ref[...])
        m_ref[...] = m_new

    @pl.when(j == pl.num_programs(3) - 1)
    def _():
        o = acc_ref[...] / l_ref[...]
        for r in range(GQA_R):
            o_ref[:, r * HEAD_DIM:(r + 1) * HEAD_DIM] = o[r * tq:(r + 1) * tq]


def slc_attn(qr, selneg, ks_aug, vs_b, T, tq, tk):
    M = qr.shape[0]
    B = M // T
    nt, nk = T // tq, T // tk

    def kv_idx(b, g, i, j):
        return (b * nk + jnp.minimum(j, (i * tq + tq - 1) // tk), g)

    return pl.pallas_call(
        functools.partial(_slc_body, tq=tq, tk=tk), grid=(B, KV_GROUPS, nt, nk),
        in_specs=[pl.BlockSpec((tq, GQA_R * HEAD_DIM), lambda b, g, i, j: (b * nt + i, g)),
                  pl.BlockSpec((None, tq, LANES), lambda b, g, i, j: (g, b * nt + i, 0)),
                  pl.BlockSpec((tk, 2 * HEAD_DIM), kv_idx),
                  pl.BlockSpec((tk, HEAD_DIM), kv_idx)],
        out_specs=pl.BlockSpec((tq, GQA_R * HEAD_DIM), lambda b, g, i, j: (b * nt + i, g)),
        out_shape=jax.ShapeDtypeStruct((M, Q_W), F32),
        scratch_shapes=[pltpu.VMEM((GQA_R * tq, 2 * HEAD_DIM), MXU), pltpu.VMEM((GQA_R * tq, 1), F32),
                        pltpu.VMEM((GQA_R * tq, 1), F32), pltpu.VMEM((GQA_R * tq, HEAD_DIM), F32)],
        compiler_params=_cp(("parallel", "parallel", "parallel", "arbitrary")), name="slc_attn")(
            qr, selneg, ks_aug, vs_b)


def _lane_col(x, c):
    lane = lax.broadcasted_iota(jnp.int32, x.shape, 1)
    return jnp.sum(jnp.where(lane == c, x, 0.0), axis=1, keepdims=True)


def _win_body(q_ref, k0, k1, k2, v0, v1, v2, oc_ref, os_ref, gn_ref, o_ref, *, tq):
    g, i = pl.program_id(1), pl.program_id(2)
    qs = _stack_heads(q_ref[...])
    k = jnp.concatenate([k0[...], k1[...], k2[...]], axis=0)
    v = jnp.concatenate([v0[...], v1[...], v2[...]], axis=0)
    s = _dot_nt(qs, k)
    qp = i * tq + (lax.broadcasted_iota(jnp.int32, s.shape, 0) & (tq - 1))
    kp = (i - 2) * tq + lax.broadcasted_iota(jnp.int32, s.shape, 1)
    p = _softmax_rows(s, (kp <= qp) & (kp > qp - WINDOW) & (kp >= 0))
    ow = _dot(p.astype(MXU), v)
    gates = jax.nn.sigmoid(gn_ref[...])
    for r in range(GQA_R):
        sl = slice(r * HEAD_DIM, (r + 1) * HEAD_DIM)
        h = g * GQA_R + r
        o = (_lane_col(gates, h) * oc_ref[:, sl] + _lane_col(gates, N_HEADS + h) * os_ref[:, sl]
             + _lane_col(gates, 2 * N_HEADS + h) * ow[r * tq:(r + 1) * tq])
        o_ref[:, sl] = o.astype(o_ref.dtype)


def win_attn_combine(qr, kw_b, vw_b, o_cmp, o_slc, y, T, tq):
    assert WINDOW == 2 * tq
    M = qr.shape[0]
    B = M // T
    nt = T // tq
    qspec = pl.BlockSpec((tq, GQA_R * HEAD_DIM), lambda b, g, i: (b * nt + i, g))
    kv = [pl.BlockSpec((tq, HEAD_DIM), lambda b, g, i, o=o: (b * nt + jnp.maximum(i - 2 + o, 0), g)) for o in range(3)]
    return pl.pallas_call(
        functools.partial(_win_body, tq=tq), grid=(B, KV_GROUPS, nt),
        in_specs=[qspec] + kv + kv + [qspec, qspec,
                                      pl.BlockSpec((tq, LANES), lambda b, g, i: (b * nt + i, OFF_GN // LANES))],
        out_specs=qspec, out_shape=jax.ShapeDtypeStruct((M, Q_W), MXU),
        compiler_params=_cp(("parallel", "parallel", "arbitrary")), name="win_attn")(
            qr, kw_b, kw_b, kw_b, vw_b, vw_b, vw_b, o_cmp, o_slc, y)


def _dec_cmp_body(q_ref, k_ref, v_ref, sel_ref, o_ref, idx_ref, *, qpos, n_sel):
    s = _dot_nt(q_ref[...], k_ref[...])
    cmp_end = lax.broadcasted_iota(jnp.int32, s.shape, 1) * CMP_STRIDE + (CMP_LEN - 1)
    p = _softmax_rows(s, cmp_end <= qpos)
    o_ref[...] = _dot(p.astype(MXU), v_ref[...])
    imp = _dot_f32_by_01(jnp.sum(p, axis=0, keepdims=True), sel_ref[...])
    blk = lax.broadcasted_iota(jnp.int32, imp.shape, 1)
    cur = qpos // SEL_BLOCK
    forced = (blk == 0) | (blk == cur) | (blk == cur - 1)
    valid = blk < n_sel
    score = jnp.where(forced, jnp.inf, jnp.where(blk * SEL_BLOCK <= qpos, imp, -jnp.inf))
    lane = lax.broadcasted_iota(jnp.int32, (1, LANES), 1)
    out = jnp.zeros((1, LANES), jnp.int32)
    taken = ~valid
    big = imp.shape[1]
    for k in range(SEL_TOPK):
        cand = jnp.where(taken, -jnp.inf, score)
        m = jnp.max(cand, axis=1, keepdims=True)
        pick = jnp.min(jnp.where((cand == m) & ~taken, blk, big), axis=1, keepdims=True)
        out = jnp.where(lane == k, pick, out)
        taken = taken | (blk == pick)
    idx_ref[...] = out


def dec_cmp_attn(qn4, kcmp, vcmp, qpos, n_sel):
    B = qn4.shape[0]
    nc = kcmp.shape[2]
    lanes = -(-n_sel // LANES) * LANES
    sel = _sel_matrix(nc, lanes)
    q_spec = pl.BlockSpec((None, None, GQA_R, HEAD_DIM), lambda b, g: (b, g, 0, 0))
    kv_spec = pl.BlockSpec((None, None, nc, HEAD_DIM), lambda b, g: (b, g, 0, 0))
    return pl.pallas_call(
        functools.partial(_dec_cmp_body, qpos=qpos, n_sel=n_sel), grid=(B, KV_GROUPS),
        in_specs=[q_spec, kv_spec, kv_spec, pl.BlockSpec((nc, lanes), lambda b, g: (0, 0))],
        out_specs=[q_spec, pl.BlockSpec((None, None, 1, LANES), lambda b, g: (b, g, 0, 0))],
        out_shape=[jax.ShapeDtypeStruct((B, KV_GROUPS, GQA_R, HEAD_DIM), F32),
                   jax.ShapeDtypeStruct((B, KV_GROUPS, 1, LANES), jnp.int32)],
        compiler_params=_cp(("parallel", "arbitrary")), name="dec_cmp_attn")(qn4, kcmp, vcmp, sel)


def _dec_attn_body(idx_ref, pt_ref, *refs, qpos, n_past_blocks):
    del pt_ref
    K = SEL_TOPK
    kb, vb = refs[:K], refs[K:2 * K]
    (q_ref, kn_ref, vn_ref, kwp_ref, vwp_ref, kwn_ref, vwn_ref, oc_ref, gn_ref, o_ref) = refs[2 * K:]
    b, g = pl.program_id(0), pl.program_id(1)
    q = q_ref[...]
    ks = jnp.concatenate([r[...].astype(MXU) for r in kb], axis=0)
    vs = jnp.concatenate([r[...].astype(MXU) for r in vb], axis=0)
    s = _dot_nt(q, ks)
    col = lax.broadcasted_iota(jnp.int32, s.shape, 1)
    blk_of_col = jnp.zeros(s.shape, jnp.int32)
    for k in range(K):
        blk_of_col = jnp.where(col // SEL_BLOCK == k, idx_ref[(b * KV_GROUPS + g) * K + k], blk_of_col)
    mask = (blk_of_col < n_past_blocks) & (blk_of_col * SEL_BLOCK + (col & (SEL_BLOCK - 1)) <= qpos)
    s = jnp.where(mask, s, NEG_INF)
    has_new = jnp.max(jnp.where(blk_of_col == qpos // SEL_BLOCK, 1.0, 0.0), axis=1, keepdims=True) > 0.5
    kn = kn_ref[...]
    s_new = jnp.where(has_new, jnp.sum(q.astype(F32) * kn.astype(MXU).astype(F32), axis=1, keepdims=True), NEG_INF)
    m = jnp.maximum(jnp.max(s, axis=1, keepdims=True), s_new)
    e = jnp.where(mask, jnp.exp(s - m), 0.0)
    e_new = jnp.where(has_new, jnp.exp(s_new - m), 0.0)
    d = jnp.sum(e, axis=1, keepdims=True) + e_new
    d = jnp.where(d > 0, d, 1.0)
    o_slc = (_dot(e.astype(MXU), vs) + e_new.astype(MXU).astype(F32) * vn_ref[...].astype(MXU).astype(F32)) / d
    kw = kwp_ref[...].astype(MXU)
    n_buf = kw.shape[0]
    sw = _dot_nt(q, kw)
    kp = qpos - n_buf + lax.broadcasted_iota(jnp.int32, sw.shape, 1)
    wmask = (kp > qpos - WINDOW) & (kp >= 0)
    sw = jnp.where(wmask, sw, NEG_INF)
    sw_new = jnp.sum(q.astype(F32) * kwn_ref[...].astype(MXU).astype(F32), axis=1, keepdims=True)
    mw = jnp.maximum(jnp.max(sw, axis=1, keepdims=True), sw_new)
    ew = jnp.where(wmask, jnp.exp(sw - mw), 0.0)
    ew_new = jnp.exp(sw_new - mw)
    dw = jnp.sum(ew, axis=1, keepdims=True) + ew_new
    o_win = (_dot(ew.astype(MXU), vwp_ref[...].astype(MXU))
             + ew_new.astype(MXU).astype(F32) * vwn_ref[...].astype(MXU).astype(F32)) / dw
    gates = jax.nn.sigmoid(gn_ref[...])
    rows = lax.broadcasted_iota(jnp.int32, (GQA_R, LANES), 0)
    lane = lax.broadcasted_iota(jnp.int32, (GQA_R, LANES), 1)
    gb = jnp.broadcast_to(gates, (GQA_R, LANES))
    pick = lambda base: jnp.sum(jnp.where(lane == base + g * GQA_R + rows, gb, 0.0), axis=1, keepdims=True)
    o = pick(0) * oc_ref[...] + pick(N_HEADS) * o_slc + pick(2 * N_HEADS) * o_win
    o_ref[...] = o.astype(o_ref.dtype)


def dec_attn(idx, page_ids, ks_pool3, vs_pool3, qr4, ks_new, vs_new, kw_past, vw_past, kw_new, vw_new,
             o_cmp4, gn3, qpos):
    B = qr4.shape[0]
    n_pg = page_ids.shape[0] // B
    n_past_blocks = n_pg * (PAGE // SEL_BLOCK)
    n_buf = kw_past.shape[1]
    per_page = PAGE // SEL_BLOCK

    def blk_spec(k):
        def im(b, g, idx_ref, pt_ref, k=k):
            blk = jnp.minimum(idx_ref[(b * KV_GROUPS + g) * SEL_TOPK + k], n_past_blocks - 1)
            return (pt_ref[b * n_pg + blk // per_page], blk % per_page, g)
        return pl.BlockSpec((None, SEL_BLOCK, HEAD_DIM), im)

    q_spec = pl.BlockSpec((None, None, GQA_R, HEAD_DIM), lambda b, g, i_, p_: (b, g, 0, 0))
    new_spec = pl.BlockSpec((None, 1, HEAD_DIM), lambda b, g, i_, p_: (b, 0, g))
    past_spec = pl.BlockSpec((None, n_buf, HEAD_DIM), lambda b, g, i_, p_: (b, 0, g))
    in_specs = ([blk_spec(k) for k in range(SEL_TOPK)] + [blk_spec(k) for k in range(SEL_TOPK)]
                + [q_spec, new_spec, new_spec, past_spec, past_spec, new_spec, new_spec, q_spec,
                   pl.BlockSpec((None, 1, LANES), lambda b, g, i_, p_: (b, 0, 0))])
    return pl.pallas_call(
        functools.partial(_dec_attn_body, qpos=qpos, n_past_blocks=n_past_blocks),
        grid_spec=pltpu.PrefetchScalarGridSpec(
            num_scalar_prefetch=2, grid=(B, KV_GROUPS), in_specs=in_specs, out_specs=q_spec),
        out_shape=jax.ShapeDtypeStruct((B, KV_GROUPS, GQA_R, HEAD_DIM), MXU),
        compiler_params=_cp(("parallel", "arbitrary")), name="dec_attn")(
            idx, page_ids, *([ks_pool3] * SEL_TOPK), *([vs_pool3] * SEL_TOPK),
            qr4, ks_new, vs_new, kw_past, vw_past, kw_new, vw_new, o_cmp4, gn3)


def _pack_w_in(w):
    d = w.shape[0]
    u, q, kv, gn, ga, gb = (w[:, :POOL_W], w[:, POOL_W:POOL_W + Q_W], w[:, POOL_W + Q_W:POOL_W + Q_W + 6 * KV_W],
                            w[:, 6144:6144 + N_GATES_NSA], w[:, 6192:6192 + D_MODEL], w[:, 8240:8240 + D_MODEL])
    pad = jnp.zeros((d, LANES - N_GATES_NSA), w.dtype)
    return jnp.concatenate([q, u, kv, ga, gb, gn, pad], axis=1).astype(MXU)


def _stack_w1(w1):
    r = CMP_LEN // CMP_STRIDE
    w = w1.reshape(r, CMP_STRIDE * HEAD_DIM, HEAD_DIM)
    return jnp.concatenate([w[o] for o in range(r)], axis=1).astype(MXU)


def _token_layer(x, p_i, lw, ffn, tabs, etab, tm):
    y = norm_mm(x, lw['attn_norm'], lw['w_in'], tm, PACK_TN)
    post = qk_post(y, tabs, etab, lw['q_norm'], lw['ks_norm'], lw['kw_norm'], min(tm, 256))
    a_pool, o_nsa = lw['attn'](y, post)
    tn = 512
    merged = merge(a_pool, o_nsa, lw['w_branch_pool'], lw['w_branch_nsa'], y, tm, tn)
    x = mm_res(merged, lw['w_out'], x, tm, tn)
    x = ffn(x)
    x = ple(x, lw['ple_norm'], lw['w_ple_gate'], p_i, lw['w_ple_proj'], tm, tn)
    return x, y, post


def kernel(x_prompt, x_sample, cache_k_cmp, cache_v_cmp, cache_k_slc, cache_v_slc, state_k_win, state_v_win,
           state_pool, page_table, p_prompt, p_sample, attn_norm, w_in, q_norm, kc_norm, ks_norm, kw_norm,
           w_cmp_k1, w_cmp_k2, w_cmp_v1, w_cmp_v2, pool_w, pool_scale, w_branch_pool, w_branch_nsa, w_out,
           ffn_norm, w_gate_d, w_up_d, w_down_d, w_router, w_gate_e, w_up_e, w_down_e, ple_norm, w_ple_gate,
           w_ple_proj):
    B, T, D = x_prompt.shape
    Bs = x_sample.shape[0]
    depth = w_in.shape[0]
    n_pool, n_pg = cache_k_cmp.shape[1], page_table.shape[1]
    win_buf = state_k_win.shape[2]
    past_len = n_pg * PAGE
    Mp, Ms = B * T, 16
    dff = w_gate_d.shape[-1]
    tm_p, tm_s = 1024, Ms
    tf = 512 if dff % 512 == 0 else dff
    c = lambda a: a.astype(MXU)

    tabs_p, etab_p = _rope_tables(jnp.arange(T))
    tabs_s, etab_s = _rope_tables(jnp.full((Ms,), past_len, jnp.int32))
    ident_pages = jnp.arange(B * (T // PAGE), dtype=jnp.int32).reshape(B, T // PAGE)
    cache3 = lambda a: a.reshape(depth * n_pool, PAGE, KV_W)
    ck3, cv3, cks3, cvs3 = cache3(cache_k_cmp), cache3(cache_v_cmp), cache3(cache_k_slc), cache3(cache_v_slc)
    n_sel_s = (past_len + 1 + SEL_BLOCK - 1) // SEL_BLOCK

    xp = x_prompt.reshape(Mp, D)
    xs = jnp.pad(x_sample.reshape(Bs, D), ((0, Ms - Bs), (0, 0)))
    pp = p_prompt.reshape(depth, Mp, PLE_DIM)
    ps = jnp.pad(p_sample.reshape(depth, Bs, PLE_DIM), ((0, 0), (0, Ms - Bs), (0, 0)))
    wr_pad = jnp.pad(w_router, ((0, 0), (0, 0), (0, LANES - N_EXPERTS)))
    states_p, states_s = [], []

    for li in range(depth):
        j = li // 2
        lw = dict(attn_norm=attn_norm[li], w_in=_pack_w_in(w_in[li]), q_norm=q_norm[li], ks_norm=ks_norm[li],
                  kw_norm=kw_norm[li], w_branch_pool=c(w_branch_pool[li]), w_branch_nsa=c(w_branch_nsa[li]),
                  w_out=c(w_out[li]), ple_norm=ple_norm[li], w_ple_gate=c(w_ple_gate[li]),
                  w_ple_proj=c(w_ple_proj[li]))
        w1k, w1v = _stack_w1(w_cmp_k1[li]), _stack_w1(w_cmp_v1[li])
        w2k, w2v = c(w_cmp_k2[li]), c(w_cmp_v2[li])
        pw, psc = c(pool_w[li]), pool_scale[li]
        if li % 2 == 0:
            wg, wu, wd = c(w_gate_d[j]), c(w_up_d[j]), c(w_down_d[j])

            def ffn(x, tm, wg=wg, wu=wu, wd=wd, li=li):
                act = ffn_gu(x, ffn_norm[li], wg, wu, tm, tf)
                return mm_res(act, wd, x, tm, 512)
        else:
            wg, wu, wd = c(w_gate_e[j]), c(w_up_e[j]), c(w_down_e[j])

            def ffn(x, tm, wg=wg, wu=wu, wd=wd, li=li, j=j):
                gate = router(x, ffn_norm[li], wr_pad[j], min(tm, 512))
                out = x
                for e in range(N_EXPERTS):
                    act = ffn_gu(x, ffn_norm[li], wg, wu, tm, tf, wsel=e)
                    out = mm_res(act, wd, out, tm, 512, scale=gate, col=e, wsel=e)
                return out

        def attn_p(y, post):
            qn, qr, _, _, ks_aug, kw_b, vs_b, vw_b = post
            a_pool = pool_mix(y.reshape(B, T, N_PACK), OFF_U // POOL_W, pw, psc, 512)
            y3 = y.reshape(Mp // PAGE, PAGE, N_PACK)
            pk = cmp_proj(y3, OFF_KC // KV_W, ident_pages, w1k)
            pv = cmp_proj(y3, OFF_VC // KV_W, ident_pages, w1v)
            kcmp, vcmp = cmp_mlp(pk, pv, w2k, w2v, kc_norm[li])
            o_cmp, selneg = cmp_attn(qn, kcmp, vcmp, T, 256)
            o_slc = slc_attn(qr, selneg, ks_aug, vs_b, T, 256, 512)
            o_nsa = win_attn_combine(qr, kw_b, vw_b, o_cmp, o_slc, y, T, 256)
            return a_pool, o_nsa

        lw['attn'] = attn_p
        xp, y, post = _token_layer(xp, pp[li], lw, functools.partial(ffn, tm=tm_p), tabs_p, etab_p, tm_p)
        kv5 = lambda a: a.reshape(B, T, KV_GROUPS, HEAD_DIM)
        ysl = lambda off, w: y[:, off:off + w]
        kw_r, u_p = kv5(post[3]), ysl(OFF_U, POOL_W).reshape(B, T, POOL_W)
        states_p.append((kv5(ysl(OFF_KC, KV_W)), kv5(ysl(OFF_VC, KV_W)), kv5(post[2]), kv5(ysl(OFF_VS, KV_W)),
                         kw_r[:, T - win_buf:], kv5(ysl(OFF_VW, KV_W))[:, T - win_buf:], u_p[:, T - POOL_BUF:]))

        page_ids = (page_table + li * n_pool).astype(jnp.int32)

        def attn_s(y, post):
            qn, qr, ks_r, kw_r, _, _, _, _ = post
            u_new = y[:Bs, OFF_U:OFF_U + POOL_W]
            u_ext = jnp.concatenate([state_pool[li], u_new[:, None, :]], axis=1)
            a_pool = pool_mix(u_ext, 0, pw, psc, 16).reshape(Bs, POOL_BUF + 1, POOL_W)[:, POOL_BUF]
            a_pool = jnp.pad(a_pool, ((0, Ms - Bs), (0, 0)))
            pk = cmp_proj(ck3, 0, page_ids, w1k)
            pv = cmp_proj(cv3, 0, page_ids, w1v)
            kcmp, vcmp = cmp_mlp(pk, pv, w2k, w2v, kc_norm[li])
            h4 = lambda a: a[:Bs].reshape(Bs, KV_GROUPS, GQA_R, HEAD_DIM)
            o_cmp4, idx = dec_cmp_attn(h4(qn), kcmp, vcmp, past_len, n_sel_s)
            new = lambda a: a[:Bs].reshape(Bs, 1, KV_W)
            o4 = dec_attn(idx[:, :, 0, :SEL_TOPK].reshape(-1), page_ids.reshape(-1), cks3, cvs3, h4(qr),
                          new(ks_r), new(y[:, OFF_VS:OFF_VS + KV_W]),
                          state_k_win[li].reshape(Bs, win_buf, KV_W), state_v_win[li].reshape(Bs, win_buf, KV_W),
                          new(kw_r), new(y[:, OFF_VW:OFF_VW + KV_W]), o_cmp4,
                          y[:Bs, OFF_GN:OFF_GN + LANES].reshape(Bs, 1, LANES), past_len)
            return a_pool, jnp.pad(o4.reshape(Bs, Q_W), ((0, Ms - Bs), (0, 0)))

        lw['attn'] = attn_s
        xs, y, post = _token_layer(xs, ps[li], lw, functools.partial(ffn, tm=tm_s), tabs_s, etab_s, tm_s)
        kv5s = lambda a: a[:Bs].reshape(Bs, 1, KV_GROUPS, HEAD_DIM)
        ysl = lambda off, w: y[:, off:off + w]
        k_win = jnp.concatenate([state_k_win[li], kv5s(post[3])], axis=1)[:, -win_buf:]
        v_win = jnp.concatenate([state_v_win[li], kv5s(ysl(OFF_VW, KV_W))], axis=1)[:, -win_buf:]
        pool = jnp.concatenate([state_pool[li], ysl(OFF_U, POOL_W)[:Bs, None, :]], axis=1)[:, -POOL_BUF:]
        states_s.append((kv5s(ysl(OFF_KC, KV_W)), kv5s(ysl(OFF_VC, KV_W)), kv5s(post[2]), kv5s(ysl(OFF_VS, KV_W)),
                         k_win, v_win, pool))

    outs_p = [jnp.stack(a) for a in zip(*states_p)]
    outs_s = [jnp.stack(a) for a in zip(*states_s)]
    return (xp.reshape(B, T, D), xs[:Bs].reshape(Bs, 1, D), *outs_p, *outs_s)
```

```python
import functools

import numpy as np
import jax
import jax.numpy as jnp
from jax import lax
from jax.experimental import pallas as pl
from jax.experimental.pallas import tpu as pltpu

F32 = jnp.float32
MXU = jnp.bfloat16

D_MODEL = 2048
N_HEADS = 16
HEAD_DIM = 128
KV_GROUPS = 4
GQA_R = N_HEADS // KV_GROUPS
ROPE_DIM = HEAD_DIM // 4
ROPE_THETA = 500000.0
SCALE = HEAD_DIM ** -0.5
CMP_LEN = 32
CMP_STRIDE = 16
SEL_BLOCK = 64
SEL_TOPK = 16
WINDOW = 512
POOL_GROUPS = 4
POOL_WINDOWS = (2, 4, 8, 16)
POOL_W = D_MODEL // 2
POOL_GW = POOL_W // POOL_GROUPS
POOL_BUF = max(POOL_WINDOWS) - 1
N_EXPERTS = 8
PLE_DIM = 256
RMS_EPS = 1e-6
Q_W = N_HEADS * HEAD_DIM
KV_W = KV_GROUPS * HEAD_DIM
N_GATES_NSA = 3 * N_HEADS
PAGE = 128

LANES = 128
VMEM_LIMIT_BYTES = 56 * 1024 * 1024
NEG_BIG = -30000.0
NEG_INF = -1e30

OFF_Q, OFF_U, OFF_KC, OFF_VC, OFF_KS, OFF_VS, OFF_KW, OFF_VW = 0, 2048, 3072, 3584, 4096, 4608, 5120, 5632
OFF_GA, OFF_GB, OFF_GN = 6144, 8192, 10240
N_PACK = 10368
PACK_TN = 1152


def _cp(sem):
    return pltpu.CompilerParams(dimension_semantics=sem, vmem_limit_bytes=VMEM_LIMIT_BYTES)


def _rms(x, g):
    ms = jnp.mean(x * x, axis=-1, keepdims=True)
    return x * lax.rsqrt(ms + RMS_EPS) * g


def _dot(a, b):
    return jnp.dot(a, b, preferred_element_type=F32)


def _dot_nt(a, b):
    return lax.dot_general(a, b, (((1,), (1,)), ((), ())), preferred_element_type=F32)


def _dot_f32_by_01(p, sel):
    if MXU == F32:
        return _dot(p, sel)
    hi = p.astype(MXU)
    r1 = p - hi.astype(F32)
    mid = r1.astype(MXU)
    lo = (r1 - mid.astype(F32)).astype(MXU)
    return _dot(hi, sel) + _dot(mid, sel) + _dot(lo, sel)


def _softmax_rows(s, mask):
    s = jnp.where(mask, s, NEG_INF)
    m = jnp.max(s, axis=-1, keepdims=True)
    e = jnp.where(mask, jnp.exp(s - m), 0.0)
    d = jnp.sum(e, axis=-1, keepdims=True)
    return e / jnp.where(d > 0, d, 1.0)


def _norm_mm_body(x_ref, g_ref, w_ref, o_ref, h_ref):
    @pl.when(pl.program_id(1) == 0)
    def _():
        h_ref[...] = _rms(x_ref[...], g_ref[...]).astype(MXU)

    o_ref[...] = _dot(h_ref[...], w_ref[...])


def norm_mm(x, g, w, tm, tn):
    M, K = x.shape
    N = w.shape[1]
    return pl.pallas_call(
        _norm_mm_body, grid=(M // tm, N // tn),
        in_specs=[pl.BlockSpec((tm, K), lambda i, j: (i, 0)),
                  pl.BlockSpec((1, K), lambda i, j: (0, 0)),
                  pl.BlockSpec((K, tn), lambda i, j: (0, j))],
        out_specs=pl.BlockSpec((tm, tn), lambda i, j: (i, j)),
        out_shape=jax.ShapeDtypeStruct((M, N), F32),
        scratch_shapes=[pltpu.VMEM((tm, K), MXU)],
        compiler_params=_cp(("parallel", "arbitrary")), name="norm_mm")(x, g.reshape(1, K), w)


def _mm_res_body(a_ref, w_ref, r_ref, o_ref):
    o_ref[...] = r_ref[...] + _dot(a_ref[...], w_ref[...])


def mm_res(a, w, res, tm, tn):
    M, K = a.shape
    N = res.shape[1]
    return pl.pallas_call(
        _mm_res_body, grid=(M // tm, N // tn),
        in_specs=[pl.BlockSpec((tm, K), lambda i, j: (i, 0)), pl.BlockSpec((K, tn), lambda i, j: (0, j)),
                  pl.BlockSpec((tm, tn), lambda i, j: (i, j))],
        out_specs=pl.BlockSpec((tm, tn), lambda i, j: (i, j)),
        out_shape=jax.ShapeDtypeStruct((M, N), F32),
        compiler_params=_cp(("parallel", "arbitrary")), name="mm_res")(a, w, res)


def _ffn_gu_body(x_ref, g_ref, wg_ref, wu_ref, o_ref, h_ref):
    @pl.when(pl.program_id(1) == 0)
    def _():
        h_ref[...] = _rms(x_ref[...], g_ref[...]).astype(MXU)

    h = h_ref[...]
    a = _dot(h, wg_ref[...])
    b = _dot(h, wu_ref[...])
    o_ref[...] = (a * jax.nn.sigmoid(a) * b).astype(o_ref.dtype)


def ffn_gu(x, g, wg, wu, tm, tf):
    M, K = x.shape
    dff = wg.shape[-1]
    w_spec = pl.BlockSpec((K, tf), lambda i, j: (0, j))
    return pl.pallas_call(
        _ffn_gu_body, grid=(M // tm, dff // tf),
        in_specs=[pl.BlockSpec((tm, K), lambda i, j: (i, 0)),
                  pl.BlockSpec((1, K), lambda i, j: (0, 0)), w_spec, w_spec],
        out_specs=pl.BlockSpec((tm, tf), lambda i, j: (i, j)),
        out_shape=jax.ShapeDtypeStruct((M, dff), MXU),
        scratch_shapes=[pltpu.VMEM((tm, K), MXU)],
        compiler_params=_cp(("parallel", "arbitrary")), name="ffn_gu")(x, g.reshape(1, K), wg, wu)


def _merge_body(ap_ref, on_ref, wp_ref, wn_ref, ga_ref, gb_ref, o_ref):
    yp = _dot(ap_ref[...], wp_ref[...])
    yn = _dot(on_ref[...], wn_ref[...])
    o_ref[...] = (jax.nn.sigmoid(ga_ref[...]) * yp + jax.nn.sigmoid(gb_ref[...]) * yn).astype(o_ref.dtype)


def merge(a_pool, o_nsa, wbp, wbn, y, tm, tn):
    M = a_pool.shape[0]
    ca, cb = OFF_GA // tn, OFF_GB // tn
    return pl.pallas_call(
        _merge_body, grid=(M // tm, D_MODEL // tn),
        in_specs=[pl.BlockSpec((tm, POOL_W), lambda i, j: (i, 0)),
                  pl.BlockSpec((tm, Q_W), lambda i, j: (i, 0)),
                  pl.BlockSpec((POOL_W, tn), lambda i, j: (0, j)),
                  pl.BlockSpec((Q_W, tn), lambda i, j: (0, j)),
                  pl.BlockSpec((tm, tn), lambda i, j: (i, ca + j)),
                  pl.BlockSpec((tm, tn), lambda i, j: (i, cb + j))],
        out_specs=pl.BlockSpec((tm, tn), lambda i, j: (i, j)),
        out_shape=jax.ShapeDtypeStruct((M, D_MODEL), MXU),
        compiler_params=_cp(("parallel", "arbitrary")), name="merge")(a_pool, o_nsa, wbp, wbn, y, y)


def _ple_body(x_ref, xt_ref, g_ref, wg_ref, p_ref, wp_ref, o_ref, h_ref):
    @pl.when(pl.program_id(1) == 0)
    def _():
        h_ref[...] = _rms(x_ref[...], g_ref[...]).astype(MXU)

    gate = jax.nn.sigmoid(_dot(h_ref[...], wg_ref[...]))
    proj = _dot(p_ref[...].astype(MXU), wp_ref[...])
    o_ref[...] = xt_ref[...] + gate * proj


def ple(x, g, wg, p, wp, tm, tn):
    M, K = x.shape
    return pl.pallas_call(
        _ple_body, grid=(M // tm, K // tn),
        in_specs=[pl.BlockSpec((tm, K), lambda i, j: (i, 0)),
                  pl.BlockSpec((tm, tn), lambda i, j: (i, j)),
                  pl.BlockSpec((1, K), lambda i, j: (0, 0)),
                  pl.BlockSpec((K, tn), lambda i, j: (0, j)),
                  pl.BlockSpec((tm, PLE_DIM), lambda i, j: (i, 0)),
                  pl.BlockSpec((PLE_DIM, tn), lambda i, j: (0, j))],
        out_specs=pl.BlockSpec((tm, tn), lambda i, j: (i, j)),
        out_shape=jax.ShapeDtypeStruct((M, K), F32),
        scratch_shapes=[pltpu.VMEM((tm, K), MXU)],
        compiler_params=_cp(("parallel", "arbitrary")), name="ple")(x, x, g.reshape(1, K), wg, p, wp)


def _router_body(x_ref, g_ref, w_ref, o_ref):
    h = _rms(x_ref[...], g_ref[...])
    logits = jnp.dot(h, w_ref[...], preferred_element_type=F32, precision=lax.Precision.HIGHEST)
    lane = lax.broadcasted_iota(jnp.int32, logits.shape, 1)
    logits = jnp.where(lane < N_EXPERTS, logits, NEG_INF)
    m1 = jnp.max(logits, axis=1, keepdims=True)
    i1 = jnp.min(jnp.where(logits == m1, lane, LANES), axis=1, keepdims=True)
    rest = jnp.where(lane == i1, NEG_INF, logits)
    m2 = jnp.max(rest, axis=1, keepdims=True)
    i2 = jnp.min(jnp.where(rest == m2, lane, LANES), axis=1, keepdims=True)
    e2 = jnp.exp(m2 - m1)
    w1 = 1.0 / (1.0 + e2)
    w2 = e2 / (1.0 + e2)
    o_ref[...] = (jnp.where(lane == i1, w1, 0.0) + jnp.where(lane == i2, w2, 0.0)
                  + jnp.where(lane == N_EXPERTS, i1.astype(F32), 0.0)
                  + jnp.where(lane == N_EXPERTS + 1, i2.astype(F32), 0.0))


def router(x, g, wr_pad, tm):
    M, K = x.shape
    return pl.pallas_call(
        _router_body, grid=(M // tm,),
        in_specs=[pl.BlockSpec((tm, K), lambda i: (i, 0)),
                  pl.BlockSpec((1, K), lambda i: (0, 0)),
                  pl.BlockSpec((K, LANES), lambda i: (0, 0))],
        out_specs=pl.BlockSpec((tm, LANES), lambda i: (i, 0)),
        out_shape=jax.ShapeDtypeStruct((M, LANES), F32),
        compiler_params=_cp(("parallel",)), name="router")(x, g.reshape(1, K), wr_pad)


MOE_TM = 512
MOE_TN = 1024
GATHER_TG = 256


def _row_copy(src_ref, row, dst_ref, r, sem):
    return pltpu.make_async_copy(src_ref.at[pl.ds(row, 1)], dst_ref.at[pl.ds(r, 1)], sem)


def _gather_into(idx_ref, base, src_ref, dst_ref, sem, n):
    def issue(r, c):
        _row_copy(src_ref, idx_ref[base + r], dst_ref, r, sem).start()
        return c

    def drain(r, c):
        _row_copy(src_ref, 0, dst_ref, r, sem).wait()
        return c

    lax.fori_loop(0, n, issue, 0)
    lax.fori_loop(0, n, drain, 0)


def _gather_body(idx_ref, src_ref, o_ref, sem, *, tg):
    _gather_into(idx_ref, pl.program_id(0) * tg, src_ref, o_ref, sem, tg)


def gather_rows(src, idx, tg):
    R, D = idx.shape[0], src.shape[1]
    return pl.pallas_call(
        functools.partial(_gather_body, tg=tg),
        grid_spec=pltpu.PrefetchScalarGridSpec(
            num_scalar_prefetch=1, grid=(R // tg,), in_specs=[pl.BlockSpec(memory_space=pl.ANY)],
            out_specs=pl.BlockSpec((tg, D), lambda i, idx: (i, 0)),
            scratch_shapes=[pltpu.SemaphoreType.DMA(())]),
        out_shape=jax.ShapeDtypeStruct((R, D), src.dtype),
        compiler_params=_cp(("arbitrary",)), name="gather_rows")(idx, src)


def _gather_add_body(i1_ref, i2_ref, x_ref, y_ref, o_ref, b1_ref, b2_ref, sem, *, tg):
    base = pl.program_id(0) * tg
    _gather_into(i1_ref, base, y_ref, b1_ref, sem.at[0], tg)
    _gather_into(i2_ref, base, y_ref, b2_ref, sem.at[1], tg)
    o_ref[...] = x_ref[...] + b1_ref[...] + b2_ref[...]


def gather_add(x, y, i1, i2, tg):
    M, D = x.shape
    row = pl.BlockSpec((tg, D), lambda i, a, b: (i, 0))
    return pl.pallas_call(
        functools.partial(_gather_add_body, tg=tg),
        grid_spec=pltpu.PrefetchScalarGridSpec(
            num_scalar_prefetch=2, grid=(M // tg,), in_specs=[row, pl.BlockSpec(memory_space=pl.ANY)],
            out_specs=row,
            scratch_shapes=[pltpu.VMEM((tg, D), F32), pltpu.VMEM((tg, D), F32), pltpu.SemaphoreType.DMA((2,))]),
        out_shape=jax.ShapeDtypeStruct((M, D), F32),
        compiler_params=_cp(("arbitrary",)), name="gather_add")(i1, i2, x, y)


def _moe_gu_body(te_ref, nu_ref, x_ref, g_ref, wg_ref, wu_ref, o_ref):
    del te_ref
    t = pl.program_id(1)

    @pl.when(t < nu_ref[0])
    def _():
        h = _rms(x_ref[...], g_ref[...]).astype(MXU)
        a = _dot(h, wg_ref[...])
        b = _dot(h, wu_ref[...])
        o_ref[...] = (a * jax.nn.sigmoid(a) * b).astype(o_ref.dtype)

    @pl.when(t >= nu_ref[0])
    def _():
        o_ref[...] = jnp.zeros(o_ref.shape, o_ref.dtype)


def moe_gu(xg, g, wg, wu, tile_expert, n_used):
    R, K = xg.shape
    dff = wg.shape[-1]
    tf = dff // 4
    last = lambda t, nu: jnp.minimum(t, nu[0] - 1)
    w_spec = pl.BlockSpec((None, K, tf), lambda f, t, te, nu: (te[last(t, nu)], 0, f))
    return pl.pallas_call(
        _moe_gu_body,
        grid_spec=pltpu.PrefetchScalarGridSpec(
            num_scalar_prefetch=2, grid=(dff // tf, R // MOE_TM),
            in_specs=[pl.BlockSpec((MOE_TM, K), lambda f, t, te, nu: (last(t, nu), 0)),
                      pl.BlockSpec((1, K), lambda f, t, te, nu: (0, 0)), w_spec, w_spec],
            out_specs=pl.BlockSpec((MOE_TM, tf), lambda f, t, te, nu: (t, f))),
        out_shape=jax.ShapeDtypeStruct((R, dff), MXU),
        compiler_params=_cp(("arbitrary", "arbitrary")), name="moe_gu")(
            tile_expert, n_used, xg, g.reshape(1, K), wg, wu)


def _moe_down_body(te_ref, nu_ref, a_ref, w_ref, s_ref, o_ref):
    del te_ref
    t = pl.program_id(1)

    @pl.when(t < nu_ref[0])
    def _():
        o_ref[...] = s_ref[...] * _dot(a_ref[...], w_ref[...])

    @pl.when(t >= nu_ref[0])
    def _():
        o_ref[...] = jnp.zeros(o_ref.shape, o_ref.dtype)


def moe_down(act, wd, row_scale, tile_expert, n_used):
    R, dff = act.shape
    D = wd.shape[-1]
    tn = min(MOE_TN, D)
    last = lambda t, nu: jnp.minimum(t, nu[0] - 1)
    return pl.pallas_call(
        _moe_down_body,
        grid_spec=pltpu.PrefetchScalarGridSpec(
            num_scalar_prefetch=2, grid=(D // tn, R // MOE_TM),
            in_specs=[pl.BlockSpec((MOE_TM, dff), lambda n, t, te, nu: (last(t, nu), 0)),
                      pl.BlockSpec((None, dff, tn), lambda n, t, te, nu: (te[last(t, nu)], 0, n)),
                      pl.BlockSpec((MOE_TM, 1), lambda n, t, te, nu: (last(t, nu), 0))],
            out_specs=pl.BlockSpec((MOE_TM, tn), lambda n, t, te, nu: (t, n))),
        out_shape=jax.ShapeDtypeStruct((R, D), F32),
        compiler_params=_cp(("arbitrary", "arbitrary")), name="moe_down")(
            tile_expert, n_used, act, wd, row_scale)


def _route(gate):
    M = gate.shape[0]
    n_tiles = (2 * M) // MOE_TM + N_EXPERTS
    R = n_tiles * MOE_TM
    e12 = gate[:, N_EXPERTS:N_EXPERTS + 2].astype(jnp.int32)
    w12 = jnp.take_along_axis(gate[:, :N_EXPERTS], e12, axis=1)
    sel = jnp.sum((e12[:, :, None] == jnp.arange(N_EXPERTS)[None, None, :]).astype(jnp.int32), axis=1)
    cnt = jnp.sum(sel, axis=0)
    padded = (cnt + MOE_TM - 1) // MOE_TM * MOE_TM
    ends = jnp.cumsum(padded)
    dest_e = (ends - padded)[None, :] + jnp.cumsum(sel, axis=0) - sel
    dest = jnp.take_along_axis(dest_e, e12, axis=1).astype(jnp.int32)
    tok = jnp.broadcast_to(jnp.arange(M, dtype=jnp.int32)[:, None], (M, 2))
    row_token = jnp.zeros((R,), jnp.int32).at[dest.reshape(-1)].set(tok.reshape(-1))
    row_scale = jnp.zeros((R,), F32).at[dest.reshape(-1)].set(w12.reshape(-1))
    tile_expert = jnp.searchsorted(ends, jnp.arange(n_tiles, dtype=jnp.int32) * MOE_TM, side='right')
    tile_expert = jnp.minimum(tile_expert, N_EXPERTS - 1).astype(jnp.int32)
    n_used = (ends[-1] // MOE_TM).astype(jnp.int32).reshape(1)
    return row_token, row_scale.reshape(R, 1), tile_expert, n_used, dest


def _rope(xh, c, s1, s2):
    return xh * c + pltpu.roll(xh, HEAD_DIM - ROPE_DIM // 2, 1) * s1 + pltpu.roll(xh, ROPE_DIM // 2, 1) * s2


VT_TILE = 256


def _qk_post_body(q_ref, ks_ref, vs_ref, kw_ref, vw_ref, c_ref, s1_ref, s2_ref, e_ref, gq_ref, gs_ref, gw_ref,
                  qn_ref, qr_ref, ksr_ref, kwr_ref, ksa_ref, kwb_ref, *vt_refs):
    c, s1, s2 = c_ref[...], s1_ref[...], s2_ref[...]
    for h in range(N_HEADS):
        sl = slice(h * HEAD_DIM, (h + 1) * HEAD_DIM)
        qn = _rms(q_ref[:, sl], gq_ref[...])
        qn_ref[:, sl] = (qn * SCALE).astype(qn_ref.dtype)
        qr_ref[:, sl] = (_rope(qn, c, s1, s2) * SCALE).astype(qr_ref.dtype)
    for g in range(KV_GROUPS):
        sl = slice(g * HEAD_DIM, (g + 1) * HEAD_DIM)
        ks = _rope(_rms(ks_ref[:, sl], gs_ref[...]), c, s1, s2)
        kw = _rope(_rms(kw_ref[:, sl], gw_ref[...]), c, s1, s2)
        ksr_ref[:, sl] = ks
        kwr_ref[:, sl] = kw
        ksa_ref[:, 2 * g * HEAD_DIM:(2 * g + 1) * HEAD_DIM] = ks.astype(ksa_ref.dtype)
        ksa_ref[:, (2 * g + 1) * HEAD_DIM:(2 * g + 2) * HEAD_DIM] = e_ref[...]
        kwb_ref[:, sl] = kw.astype(kwb_ref.dtype)
        if vt_refs:
            for v_ref, vt_ref in ((vs_ref, vt_refs[0]), (vw_ref, vt_refs[1])):
                for h in range(vt_ref.shape[1]):
                    vt_ref[g, h] = jnp.transpose(v_ref[h * VT_TILE:(h + 1) * VT_TILE, sl]).astype(vt_ref.dtype)


def qk_post(y, tabs, etab, gq, gs, gw, tm, with_vt):
    M = y.shape[0]
    ntb = tabs[0].shape[0] // tm
    kvb = lambda off: pl.BlockSpec((tm, KV_W), lambda i, off=off: (i, off // KV_W))
    tab = pl.BlockSpec((tm, HEAD_DIM), lambda i: (i % ntb, 0))
    gain = pl.BlockSpec((1, HEAD_DIM), lambda i: (0, 0))
    row = lambda w: pl.BlockSpec((tm, w), lambda i: (i, 0))
    sds = jax.ShapeDtypeStruct
    out_specs = [row(Q_W), row(Q_W), row(KV_W), row(KV_W), row(2 * KV_W), row(KV_W)]
    out_shape = [sds((M, Q_W), MXU), sds((M, Q_W), MXU), sds((M, KV_W), F32), sds((M, KV_W), F32),
                 sds((M, 2 * KV_W), MXU), sds((M, KV_W), MXU)]
    if with_vt:
        per = tm // VT_TILE
        vt_spec = pl.BlockSpec((KV_GROUPS, per, HEAD_DIM, VT_TILE), lambda i: (0, i, 0, 0))
        out_specs += [vt_spec, vt_spec]
        out_shape += [sds((KV_GROUPS, M // VT_TILE, HEAD_DIM, VT_TILE), MXU)] * 2
    return pl.pallas_call(
        _qk_post_body, grid=(M // tm,),
        in_specs=[pl.BlockSpec((tm, Q_W), lambda i: (i, OFF_Q // Q_W)), kvb(OFF_KS), kvb(OFF_VS), kvb(OFF_KW),
                  kvb(OFF_VW), tab, tab, tab, tab, gain, gain, gain],
        out_specs=out_specs, out_shape=out_shape,
        compiler_params=_cp(("parallel",)), name="qk_post")(
            y, y, y, y, y, tabs[0], tabs[1], tabs[2], etab,
            gq.reshape(1, HEAD_DIM), gs.reshape(1, HEAD_DIM), gw.reshape(1, HEAD_DIM))


def _rope_tables(pos):
    half = ROPE_DIM // 2
    n = pos.shape[0]
    inv = jnp.power(jnp.float32(ROPE_THETA), -jnp.arange(half, dtype=F32) * (2.0 / ROPE_DIM))
    ang = pos.astype(F32)[:, None] * inv[None, :]
    cos, sin = jnp.cos(ang), jnp.sin(ang)
    z = lambda w: jnp.zeros((n, w), F32)
    c = jnp.concatenate([cos, cos, jnp.ones((n, HEAD_DIM - ROPE_DIM), F32)], axis=1)
    s1 = jnp.concatenate([-sin, z(HEAD_DIM - half)], axis=1)
    s2 = jnp.concatenate([z(half), sin, z(HEAD_DIM - ROPE_DIM)], axis=1)
    etab = (pos[:, None] // SEL_BLOCK == jnp.arange(HEAD_DIM)[None, :]).astype(MXU)
    return (c, s1, s2), etab


def _pool_body(u_ref, halo_ref, w_ref, sc_ref, o_ref, *, tp):
    i = pl.program_id(1)
    u = u_ref[...]
    first = (lax.broadcasted_iota(jnp.int32, halo_ref.shape, 0) * 0 + i) == 0
    prev = jnp.where(first, 0.0, halo_ref[...])
    ext = jnp.concatenate([prev, u], axis=0)
    a2 = ext[1:] + ext[:-1]
    a4 = a2[2:, POOL_GW:] + a2[:-2, POOL_GW:]
    a8 = a4[4:, POOL_GW:] + a4[:-4, POOL_GW:]
    a16 = a8[8:, POOL_GW:] + a8[:-8, POOL_GW:]
    sums = (a2[15:, :POOL_GW], a4[13:, :POOL_GW], a8[9:, :POOL_GW], a16[1:])
    t = (i * tp + lax.broadcasted_iota(jnp.int32, (tp, 1), 0) + 1).astype(F32)
    for g, w in enumerate(POOL_WINDOWS):
        sl = slice(g * POOL_GW, (g + 1) * POOL_GW)
        cnt = jnp.minimum(t, float(w))
        z = (sums[g] / cnt - u[:, sl]).astype(MXU)
        o_ref[:, sl] = (_dot(z, w_ref[g]) * sc_ref[:, sl]).astype(o_ref.dtype)


def pool_mix(u3, col, pool_w, pool_scale, tp):
    B, T, _ = u3.shape
    hb = tp // 16
    out = pl.pallas_call(
        functools.partial(_pool_body, tp=tp), grid=(B, T // tp),
        in_specs=[pl.BlockSpec((None, tp, POOL_W), lambda b, i: (b, i, col)),
                  pl.BlockSpec((None, 16, POOL_W), lambda b, i: (b, jnp.maximum(i * hb - 1, 0), col)),
                  pl.BlockSpec((POOL_GROUPS, POOL_GW, POOL_GW), lambda b, i: (0, 0, 0)),
                  pl.BlockSpec((1, POOL_W), lambda b, i: (0, 0))],
        out_specs=pl.BlockSpec((None, tp, POOL_W), lambda b, i: (b, i, 0)),
        out_shape=jax.ShapeDtypeStruct((B, T, POOL_W), MXU),
        compiler_params=_cp(("parallel", "arbitrary")), name="pool_mix")(
            u3, u3, pool_w, pool_scale.reshape(1, POOL_W))
    return out.reshape(B * T, POOL_W)


CMP_PAGES = 32


def _cmp_proj_body(pt_ref, *refs):
    del pt_ref
    pages, w_ref, o_ref = refs[:-2], refs[-2], refs[-1]
    per_page = PAGE // CMP_STRIDE
    cols = []
    for j in range(CMP_STRIDE):
        rows = [pg[pl.ds(j, per_page, stride=CMP_STRIDE), :] for pg in pages]
        cols.append(jnp.concatenate(rows, axis=0).astype(MXU))
    lhs = jnp.concatenate(cols, axis=1)
    o_ref[...] = _dot(lhs, w_ref[...])


def cmp_proj(pages3, col, page_ids, w1s):
    B, n_pg = page_ids.shape
    per_page = PAGE // CMP_STRIDE
    pps = min(CMP_PAGES, n_pg)
    steps = n_pg // pps
    in_specs = [pl.BlockSpec((None, PAGE, HEAD_DIM),
                             lambda b, s, g, pt, k=k: (pt[b * n_pg + s * pps + k], 0, col * KV_GROUPS + g))
                for k in range(pps)]
    in_specs.append(pl.BlockSpec((CMP_STRIDE * HEAD_DIM, 2 * HEAD_DIM), lambda b, s, g, pt: (0, 0)))
    rows = pps * per_page
    return pl.pallas_call(
        _cmp_proj_body,
        grid_spec=pltpu.PrefetchScalarGridSpec(
            num_scalar_prefetch=1, grid=(B, steps, KV_GROUPS), in_specs=in_specs,
            out_specs=pl.BlockSpec((None, rows, 2 * HEAD_DIM), lambda b, s, g, pt: (b, s, g))),
        out_shape=jax.ShapeDtypeStruct((B, n_pg * per_page, KV_GROUPS * 2 * HEAD_DIM), F32),
        compiler_params=_cp(("parallel", "arbitrary", "arbitrary")), name="cmp_proj")(
            page_ids.reshape(-1), *([pages3] * pps), w1s)


def _cmp_proj_cache_body(pt_ref, *refs):
    del pt_ref
    pages, w_ref, o_ref = refs[:-2], refs[-2], refs[-1]
    per_page = PAGE // CMP_STRIDE
    for g in range(KV_GROUPS):
        cols = []
        for j in range(CMP_STRIDE):
            rows = [pg[pl.ds(j, per_page, stride=CMP_STRIDE), g, :] for pg in pages]
            cols.append(jnp.concatenate(rows, axis=0).astype(MXU))
        lhs = jnp.concatenate(cols, axis=1)
        o_ref[:, g * 2 * HEAD_DIM:(g + 1) * 2 * HEAD_DIM] = _dot(lhs, w_ref[...])


def cmp_proj_cache(cache, li, page_table, w1s):
    B, n_pg = page_table.shape
    per_page = PAGE // CMP_STRIDE
    pps = min(CMP_PAGES, n_pg)
    steps = n_pg // pps
    in_specs = [pl.BlockSpec((None, None, PAGE, KV_GROUPS, HEAD_DIM),
                             lambda b, s, pt, k=k: (li, pt[b * n_pg + s * pps + k], 0, 0, 0))
                for k in range(pps)]
    in_specs.append(pl.BlockSpec((CMP_STRIDE * HEAD_DIM, 2 * HEAD_DIM), lambda b, s, pt: (0, 0)))
    rows = pps * per_page
    return pl.pallas_call(
        _cmp_proj_cache_body,
        grid_spec=pltpu.PrefetchScalarGridSpec(
            num_scalar_prefetch=1, grid=(B, steps), in_specs=in_specs,
            out_specs=pl.BlockSpec((None, rows, KV_GROUPS * 2 * HEAD_DIM), lambda b, s, pt: (b, s, 0))),
        out_shape=jax.ShapeDtypeStruct((B, n_pg * per_page, KV_GROUPS * 2 * HEAD_DIM), F32),
        compiler_params=_cp(("parallel", "arbitrary")), name="cmp_proj_cache")(
            page_table.reshape(-1), *([cache] * pps), w1s)


def _cmp_mlp_body(pk_ref, pv_ref, w2k_ref, w2v_ref, g_ref, ko_ref, vo_ref):
    def one(p_ref, w2_ref, g):
        p0 = p_ref[:, 2 * g * HEAD_DIM:(2 * g + 1) * HEAD_DIM]
        p1 = p_ref[:, (2 * g + 1) * HEAD_DIM:(2 * g + 2) * HEAD_DIM]
        pre = p0 + jnp.concatenate([p1[1:], jnp.zeros((1, HEAD_DIM), F32)], axis=0)
        return _dot(jax.nn.gelu(pre).astype(MXU), w2_ref[...])

    for g in range(KV_GROUPS):
        ko_ref[g] = _rms(one(pk_ref, w2k_ref, g), g_ref[...]).astype(ko_ref.dtype)
        vo_ref[g] = one(pv_ref, w2v_ref, g).astype(vo_ref.dtype)


def cmp_mlp(pk, pv, w2k, w2v, kc_g):
    B, n, w = pk.shape
    p_spec = pl.BlockSpec((None, n, w), lambda b: (b, 0, 0))
    w_spec = pl.BlockSpec((HEAD_DIM, HEAD_DIM), lambda b: (0, 0))
    o_spec = pl.BlockSpec((None, KV_GROUPS, n, HEAD_DIM), lambda b: (b, 0, 0, 0))
    sd = jax.ShapeDtypeStruct((B, KV_GROUPS, n, HEAD_DIM), MXU)
    return pl.pallas_call(
        _cmp_mlp_body, grid=(B,),
        in_specs=[p_spec, p_spec, w_spec, w_spec, pl.BlockSpec((1, HEAD_DIM), lambda b: (0, 0))],
        out_specs=[o_spec, o_spec], out_shape=[sd, sd],
        compiler_params=_cp(("parallel",)), name="cmp_mlp")(pk, pv, w2k, w2v, kc_g.reshape(1, HEAD_DIM))


def _sel_matrix(n_cmp_rows, n_lanes):
    per = SEL_BLOCK // CMP_STRIDE
    c = np.arange(n_cmp_rows)[:, None]
    j = np.arange(n_lanes)[None, :]
    lo = per * j - (CMP_LEN // CMP_STRIDE - 1)
    return jnp.asarray((c >= lo) & (c <= per * j + per - 1), dtype=MXU)


def _stack_heads(x):
    return jnp.concatenate([x[:, r * HEAD_DIM:(r + 1) * HEAD_DIM] for r in range(GQA_R)], axis=0)


def _cmp_attn_body(q_ref, k_ref, v_ref, sel_ref, o_ref, sn_ref, *, tq, n_sel):
    i = pl.program_id(2)
    qs = _stack_heads(q_ref[...])
    s = _dot_nt(qs, k_ref[...])
    nc = s.shape[1]
    t_rows = i * tq + (lax.broadcasted_iota(jnp.int32, s.shape, 0) & (tq - 1))
    cmp_end = lax.broadcasted_iota(jnp.int32, s.shape, 1) * CMP_STRIDE + (CMP_LEN - 1)
    p = _softmax_rows(s, cmp_end <= t_rows)
    o = _dot(p.astype(MXU), v_ref[...])
    for r in range(GQA_R):
        o_ref[:, r * HEAD_DIM:(r + 1) * HEAD_DIM] = o[r * tq:(r + 1) * tq]
    p_sum = p[0:tq] + p[tq:2 * tq] + p[2 * tq:3 * tq] + p[3 * tq:4 * tq]
    imp = _dot_f32_by_01(p_sum, sel_ref[...])
    t = i * tq + lax.broadcasted_iota(jnp.int32, imp.shape, 0)
    blk = lax.broadcasted_iota(jnp.int32, imp.shape, 1)
    cur = t // SEL_BLOCK
    forced = (blk == 0) | (blk == cur) | (blk == cur - 1)
    score = jnp.where(forced, jnp.inf, jnp.where(blk * SEL_BLOCK <= t, imp, -jnp.inf))
    st = jnp.transpose(score)[:n_sel]
    nb = n_sel // 8
    tiles = [st[8 * a:8 * a + 8] for a in range(nb)]
    ranks = [jnp.zeros((8, tq), F32) for _ in range(nb)]
    sub = lax.broadcasted_iota(jnp.int32, (8, tq), 0)
    for b in range(n_sel):
        row = jnp.broadcast_to(st[b:b + 1], (8, tq))
        for a in range(nb):
            ge = jnp.where(row >= tiles[a], 1.0, 0.0)
            gt = jnp.where(row > tiles[a], 1.0, 0.0)
            if 8 * a > b:
                beats = ge
            elif 8 * a + 7 < b:
                beats = gt
            else:
                beats = jnp.where(sub + 8 * a > b, ge, gt)
            ranks[a] = ranks[a] + beats
    neg = [jnp.where(r < float(SEL_TOPK), 0.0, NEG_BIG) for r in ranks]
    neg_t = jnp.concatenate(neg + [jnp.zeros((LANES - n_sel, tq), F32)], axis=0)
    sn_ref[...] = jnp.transpose(neg_t).astype(sn_ref.dtype)


def cmp_attn(qn, kcmp, vcmp, T, tq):
    M = qn.shape[0]
    B = M // T
    nc = kcmp.shape[2]
    n_sel = T // SEL_BLOCK
    nt = T // tq
    sel = _sel_matrix(nc, LANES)
    kv_spec = pl.BlockSpec((None, None, nc, HEAD_DIM), lambda b, g, i: (b, g, 0, 0))
    return pl.pallas_call(
        functools.partial(_cmp_attn_body, tq=tq, n_sel=n_sel), grid=(B, KV_GROUPS, nt),
        in_specs=[pl.BlockSpec((tq, GQA_R * HEAD_DIM), lambda b, g, i: (b * nt + i, g)), kv_spec, kv_spec,
                  pl.BlockSpec((nc, LANES), lambda b, g, i: (0, 0))],
        out_specs=[pl.BlockSpec((tq, GQA_R * HEAD_DIM), lambda b, g, i: (b * nt + i, g)),
                   pl.BlockSpec((None, tq, LANES), lambda b, g, i: (g, b * nt + i, 0))],
        out_shape=[jax.ShapeDtypeStruct((M, Q_W), F32), jax.ShapeDtypeStruct((KV_GROUPS, M, LANES), MXU)],
        compiler_params=_cp(("parallel", "parallel", "arbitrary")), name="cmp_attn")(qn, kcmp, vcmp, sel)


def _slc_body(q_ref, sn_ref, k_ref, vt_ref, o_ref, qa_ref, m_ref, l_ref, acc_ref, *, tq, tk):
    i = pl.program_id(2)
    sn = sn_ref[...]
    for r in range(GQA_R):
        qa_ref[r * tq:(r + 1) * tq, :HEAD_DIM] = q_ref[:, r * HEAD_DIM:(r + 1) * HEAD_DIM]
        qa_ref[r * tq:(r + 1) * tq, HEAD_DIM:] = sn
    m_ref[...] = jnp.full(m_ref.shape, NEG_INF, F32)
    l_ref[...] = jnp.zeros(l_ref.shape, F32)
    acc_ref[...] = jnp.zeros(acc_ref.shape, F32)
    last = (i * tq + tq - 1) // tk
    per = tk // VT_TILE

    def step(j, causal):
        k = k_ref[pl.ds(pl.multiple_of(j * tk, tk), tk), :]
        st = _dot_nt(k, qa_ref[...])
        if causal:
            kpos = j * tk + lax.broadcasted_iota(jnp.int32, st.shape, 0)
            t_cols = i * tq + (lax.broadcasted_iota(jnp.int32, st.shape, 1) & (tq - 1))
            st = jnp.where(kpos <= t_cols, st, NEG_INF)
        m_old = m_ref[...]
        m_new = jnp.maximum(m_old, jnp.max(st, axis=0, keepdims=True))
        alpha = jnp.exp(m_old - m_new)
        p = jnp.exp(st - m_new)
        l_ref[...] = alpha * l_ref[...] + jnp.sum(p, axis=0, keepdims=True)
        pb = p.astype(MXU)
        pv = _dot(vt_ref[j * per], pb[:VT_TILE])
        for h in range(1, per):
            pv = pv + _dot(vt_ref[j * per + h], pb[h * VT_TILE:(h + 1) * VT_TILE])
        acc_ref[...] = alpha * acc_ref[...] + pv
        m_ref[...] = m_new

    def body(j, carry):
        step(j, False)
        return carry

    lax.fori_loop(0, last, body, 0)
    step(last, True)
    o = acc_ref[...] / l_ref[...]
    for r in range(GQA_R):
        o_ref[:, r * HEAD_DIM:(r + 1) * HEAD_DIM] = jnp.transpose(o[:, r * tq:(r + 1) * tq])


def slc_attn(qr, selneg, ks_aug, vs_t, T, tq, tk):
    M = qr.shape[0]
    B = M // T
    nt = T // tq
    per_b = T // VT_TILE
    return pl.pallas_call(
        functools.partial(_slc_body, tq=tq, tk=tk), grid=(B, KV_GROUPS, nt),
        in_specs=[pl.BlockSpec((tq, GQA_R * HEAD_DIM), lambda b, g, i: (b * nt + i, g)),
                  pl.BlockSpec((None, tq, LANES), lambda b, g, i: (g, b * nt + i, 0)),
                  pl.BlockSpec((T, 2 * HEAD_DIM), lambda b, g, i: (b, g)),
                  pl.BlockSpec((None, per_b, HEAD_DIM, VT_TILE), lambda b, g, i: (g, b, 0, 0))],
        out_specs=pl.BlockSpec((tq, GQA_R * HEAD_DIM), lambda b, g, i: (b * nt + i, g)),
        out_shape=jax.ShapeDtypeStruct((M, Q_W), F32),
        scratch_shapes=[pltpu.VMEM((GQA_R * tq, 2 * HEAD_DIM), MXU), pltpu.VMEM((1, GQA_R * tq), F32),
                        pltpu.VMEM((1, GQA_R * tq), F32), pltpu.VMEM((HEAD_DIM, GQA_R * tq), F32)],
        compiler_params=_cp(("parallel", "parallel", "arbitrary")), name="slc_attn")(
            qr, selneg, ks_aug, vs_t)


def _lane_col(x, c):
    lane = lax.broadcasted_iota(jnp.int32, x.shape, 1)
    return jnp.sum(jnp.where(lane == c, x, 0.0), axis=1, keepdims=True)


def _win_body(q_ref, k0, k1, k2, vt0, vt1, vt2, oc_ref, os_ref, gn_ref, o_ref, *, tq):
    g, i = pl.program_id(1), pl.program_id(2)
    qs = _stack_heads(q_ref[...])
    k = jnp.concatenate([k0[...], k1[...], k2[...]], axis=0)
    st = _dot_nt(k, qs)
    kp = (i - 2) * tq + lax.broadcasted_iota(jnp.int32, st.shape, 0)
    qp = i * tq + (lax.broadcasted_iota(jnp.int32, st.shape, 1) & (tq - 1))
    st = jnp.where((kp <= qp) & (kp > qp - WINDOW) & (kp >= 0), st, NEG_INF)
    m = jnp.max(st, axis=0, keepdims=True)
    e = jnp.exp(st - m)
    l = jnp.sum(e, axis=0, keepdims=True)
    eb = e.astype(MXU)
    ot = (_dot(vt0[...], eb[:tq]) + _dot(vt1[...], eb[tq:2 * tq]) + _dot(vt2[...], eb[2 * tq:])) / l
    gates = jax.nn.sigmoid(gn_ref[...])
    for r in range(GQA_R):
        sl = slice(r * HEAD_DIM, (r + 1) * HEAD_DIM)
        h = g * GQA_R + r
        ow = jnp.transpose(ot[:, r * tq:(r + 1) * tq])
        o = (_lane_col(gates, h) * oc_ref[:, sl] + _lane_col(gates, N_HEADS + h) * os_ref[:, sl]
             + _lane_col(gates, 2 * N_HEADS + h) * ow)
        o_ref[:, sl] = o.astype(o_ref.dtype)


def win_attn_combine(qr, kw_b, vw_t, o_cmp, o_slc, y, T, tq):
    assert WINDOW == 2 * tq and tq == VT_TILE
    M = qr.shape[0]
    B = M // T
    nt = T // tq
    qspec = pl.BlockSpec((tq, GQA_R * HEAD_DIM), lambda b, g, i: (b * nt + i, g))
    kv = [pl.BlockSpec((tq, HEAD_DIM), lambda b, g, i, o=o: (b * nt + jnp.maximum(i - 2 + o, 0), g)) for o in range(3)]
    vt = [pl.BlockSpec((None, None, HEAD_DIM, tq), lambda b, g, i, o=o: (g, b * nt + jnp.maximum(i - 2 + o, 0), 0, 0))
          for o in range(3)]
    return pl.pallas_call(
        functools.partial(_win_body, tq=tq), grid=(B, KV_GROUPS, nt),
        in_specs=[qspec] + kv + vt + [qspec, qspec,
                                      pl.BlockSpec((tq, LANES), lambda b, g, i: (b * nt + i, OFF_GN // LANES))],
        out_specs=qspec, out_shape=jax.ShapeDtypeStruct((M, Q_W), MXU),
        compiler_params=_cp(("parallel", "parallel", "arbitrary")), name="win_attn")(
            qr, kw_b, kw_b, kw_b, vw_t, vw_t, vw_t, o_cmp, o_slc, y)


def _dec_cmp_body(q_ref, k_ref, v_ref, sel_ref, o_ref, idx_ref, *, qpos, n_sel):
    s = _dot_nt(q_ref[...], k_ref[...])
    cmp_end = lax.broadcasted_iota(jnp.int32, s.shape, 1) * CMP_STRIDE + (CMP_LEN - 1)
    p = _softmax_rows(s, cmp_end <= qpos)
    o_ref[...] = _dot(p.astype(MXU), v_ref[...])
    imp = _dot_f32_by_01(jnp.sum(p, axis=0, keepdims=True), sel_ref[...])
    blk = lax.broadcasted_iota(jnp.int32, imp.shape, 1)
    cur = qpos // SEL_BLOCK
    forced = (blk == 0) | (blk == cur) | (blk == cur - 1)
    valid = blk < n_sel
    score = jnp.where(forced, jnp.inf, jnp.where(blk * SEL_BLOCK <= qpos, imp, -jnp.inf))
    lane = lax.broadcasted_iota(jnp.int32, (1, LANES), 1)
    out = jnp.zeros((1, LANES), jnp.int32)
    taken = ~valid
    big = imp.shape[1]
    for k in range(SEL_TOPK):
        cand = jnp.where(taken, -jnp.inf, score)
        m = jnp.max(cand, axis=1, keepdims=True)
        pick = jnp.min(jnp.where((cand == m) & ~taken, blk, big), axis=1, keepdims=True)
        out = jnp.where(lane == k, pick, out)
        taken = taken | (blk == pick)
    idx_ref[...] = out


def dec_cmp_attn(qn4, kcmp, vcmp, qpos, n_sel):
    B = qn4.shape[0]
    nc = kcmp.shape[2]
    lanes = -(-n_sel // LANES) * LANES
    sel = _sel_matrix(nc, lanes)
    q_spec = pl.BlockSpec((None, None, GQA_R, HEAD_DIM), lambda b, g: (b, g, 0, 0))
    kv_spec = pl.BlockSpec((None, None, nc, HEAD_DIM), lambda b, g: (b, g, 0, 0))
    return pl.pallas_call(
        functools.partial(_dec_cmp_body, qpos=qpos, n_sel=n_sel), grid=(B, KV_GROUPS),
        in_specs=[q_spec, kv_spec, kv_spec, pl.BlockSpec((nc, lanes), lambda b, g: (0, 0))],
        out_specs=[q_spec, pl.BlockSpec((None, None, 1, LANES), lambda b, g: (b, g, 0, 0))],
        out_shape=[jax.ShapeDtypeStruct((B, KV_GROUPS, GQA_R, HEAD_DIM), F32),
                   jax.ShapeDtypeStruct((B, KV_GROUPS, 1, LANES), jnp.int32)],
        compiler_params=_cp(("parallel", "arbitrary")), name="dec_cmp_attn")(qn4, kcmp, vcmp, sel)


def _pick_group(ref, g):
    x = ref[:, 0, :]
    gv = lax.broadcasted_iota(jnp.int32, x.shape, 0) * 0 + g
    for q in range(1, KV_GROUPS):
        x = jnp.where(gv == q, ref[:, q, :], x)
    return x


def _dec_attn_body(idx_ref, pt_ref, *refs, qpos, n_past_blocks):
    del pt_ref
    K = SEL_TOPK
    kb, vb = refs[:K], refs[K:2 * K]
    (q_ref, kn_ref, vn_ref, kwp_ref, vwp_ref, kwn_ref, vwn_ref, oc_ref, gn_ref, o_ref) = refs[2 * K:]
    b, g = pl.program_id(0), pl.program_id(1)
    q = q_ref[...]
    ks = jnp.concatenate([_pick_group(r, g).astype(MXU) for r in kb], axis=0)
    vs = jnp.concatenate([_pick_group(r, g).astype(MXU) for r in vb], axis=0)
    s = _dot_nt(q, ks)
    col = lax.broadcasted_iota(jnp.int32, s.shape, 1)
    blk_of_col = jnp.zeros(s.shape, jnp.int32)
    for k in range(K):
        blk_of_col = jnp.where(col // SEL_BLOCK == k, idx_ref[(b * KV_GROUPS + g) * K + k], blk_of_col)
    mask = (blk_of_col < n_past_blocks) & (blk_of_col * SEL_BLOCK + (col & (SEL_BLOCK - 1)) <= qpos)
    s = jnp.where(mask, s, NEG_INF)
    has_new = jnp.max(jnp.where(blk_of_col == qpos // SEL_BLOCK, 1.0, 0.0), axis=1, keepdims=True) > 0.5
    kn = kn_ref[...]
    s_new = jnp.where(has_new, jnp.sum(q.astype(F32) * kn.astype(MXU).astype(F32), axis=1, keepdims=True), NEG_INF)
    m = jnp.maximum(jnp.max(s, axis=1, keepdims=True), s_new)
    e = jnp.where(mask, jnp.exp(s - m), 0.0)
    e_new = jnp.where(has_new, jnp.exp(s_new - m), 0.0)
    d = jnp.sum(e, axis=1, keepdims=True) + e_new
    d = jnp.where(d > 0, d, 1.0)
    o_slc = (_dot(e.astype(MXU), vs) + e_new.astype(MXU).astype(F32) * vn_ref[...].astype(MXU).astype(F32)) / d
    kw = _pick_group(kwp_ref, g).astype(MXU)
    n_buf = kw.shape[0]
    sw = _dot_nt(q, kw)
    kp = qpos - n_buf + lax.broadcasted_iota(jnp.int32, sw.shape, 1)
    wmask = (kp > qpos - WINDOW) & (kp >= 0)
    sw = jnp.where(wmask, sw, NEG_INF)
    sw_new = jnp.sum(q.astype(F32) * kwn_ref[...].astype(MXU).astype(F32), axis=1, keepdims=True)
    mw = jnp.maximum(jnp.max(sw, axis=1, keepdims=True), sw_new)
    ew = jnp.where(wmask, jnp.exp(sw - mw), 0.0)
    ew_new = jnp.exp(sw_new - mw)
    dw = jnp.sum(ew, axis=1, keepdims=True) + ew_new
    o_win = (_dot(ew.astype(MXU), _pick_group(vwp_ref, g).astype(MXU))
             + ew_new.astype(MXU).astype(F32) * vwn_ref[...].astype(MXU).astype(F32)) / dw
    gates = jax.nn.sigmoid(gn_ref[...])
    rows = lax.broadcasted_iota(jnp.int32, (GQA_R, LANES), 0)
    lane = lax.broadcasted_iota(jnp.int32, (GQA_R, LANES), 1)
    gb = jnp.broadcast_to(gates, (GQA_R, LANES))
    pick = lambda base: jnp.sum(jnp.where(lane == base + g * GQA_R + rows, gb, 0.0), axis=1, keepdims=True)
    o = pick(0) * oc_ref[...] + pick(N_HEADS) * o_slc + pick(2 * N_HEADS) * o_win
    o_ref[...] = o.astype(o_ref.dtype)


def dec_attn(idx, page_ids, ks_cache, vs_cache, li, qr4, ks_new, vs_new, kw_past, vw_past, kw_new, vw_new,
             o_cmp4, gn3, qpos):
    B = qr4.shape[0]
    n_pg = page_ids.shape[0] // B
    n_past_blocks = n_pg * (PAGE // SEL_BLOCK)
    n_buf = kw_past.shape[2]
    per_page = PAGE // SEL_BLOCK

    def blk_spec(k):
        def im(b, g, idx_ref, pt_ref, k=k):
            blk = jnp.minimum(idx_ref[(b * KV_GROUPS + g) * SEL_TOPK + k], n_past_blocks - 1)
            return (li, pt_ref[b * n_pg + blk // per_page], blk % per_page, 0, 0)
        return pl.BlockSpec((None, None, SEL_BLOCK, KV_GROUPS, HEAD_DIM), im)

    q_spec = pl.BlockSpec((None, None, GQA_R, HEAD_DIM), lambda b, g, i_, p_: (b, g, 0, 0))
    new_spec = pl.BlockSpec((None, 1, HEAD_DIM), lambda b, g, i_, p_: (b, 0, g))
    past_spec = pl.BlockSpec((None, None, n_buf, KV_GROUPS, HEAD_DIM), lambda b, g, i_, p_: (li, b, 0, 0, 0))
    in_specs = ([blk_spec(k) for k in range(SEL_TOPK)] + [blk_spec(k) for k in range(SEL_TOPK)]
                + [q_spec, new_spec, new_spec, past_spec, past_spec, new_spec, new_spec, q_spec,
                   pl.BlockSpec((None, 1, LANES), lambda b, g, i_, p_: (b, 0, 0))])
    return pl.pallas_call(
        functools.partial(_dec_attn_body, qpos=qpos, n_past_blocks=n_past_blocks),
        grid_spec=pltpu.PrefetchScalarGridSpec(
            num_scalar_prefetch=2, grid=(B, KV_GROUPS), in_specs=in_specs, out_specs=q_spec),
        out_shape=jax.ShapeDtypeStruct((B, KV_GROUPS, GQA_R, HEAD_DIM), MXU),
        compiler_params=_cp(("parallel", "arbitrary")), name="dec_attn")(
            idx, page_ids, *([ks_cache] * SEL_TOPK), *([vs_cache] * SEL_TOPK),
            qr4, ks_new, vs_new, kw_past, vw_past, kw_new, vw_new, o_cmp4, gn3)


def _pack_w_in(w):
    d = w.shape[0]
    u, q, kv, gn, ga, gb = (w[:, :POOL_W], w[:, POOL_W:POOL_W + Q_W], w[:, POOL_W + Q_W:POOL_W + Q_W + 6 * KV_W],
                            w[:, 6144:6144 + N_GATES_NSA], w[:, 6192:6192 + D_MODEL], w[:, 8240:8240 + D_MODEL])
    pad = jnp.zeros((d, LANES - N_GATES_NSA), w.dtype)
    return jnp.concatenate([q, u, kv, ga, gb, gn, pad], axis=1).astype(MXU)


def _stack_w1(w1):
    r = CMP_LEN // CMP_STRIDE
    w = w1.reshape(r, CMP_STRIDE * HEAD_DIM, HEAD_DIM)
    return jnp.concatenate([w[o] for o in range(r)], axis=1).astype(MXU)


def _mixer_half(x, lw, attn, tabs, etab, tm, with_vt):
    y = norm_mm(x, lw['attn_norm'], lw['w_in'], tm, PACK_TN)
    post = qk_post(y, tabs, etab, lw['q_norm'], lw['ks_norm'], lw['kw_norm'], min(tm, 512), with_vt)
    a_pool, o_nsa = attn(y, post)
    merged = merge(a_pool, o_nsa, lw['w_branch_pool'], lw['w_branch_nsa'], y, tm, 512)
    return mm_res(merged, lw['w_out'], x, tm, 512), y, post


def kernel(x_prompt, x_sample, cache_k_cmp, cache_v_cmp, cache_k_slc, cache_v_slc, state_k_win, state_v_win,
           state_pool, page_table, p_prompt, p_sample, attn_norm, w_in, q_norm, kc_norm, ks_norm, kw_norm,
           w_cmp_k1, w_cmp_k2, w_cmp_v1, w_cmp_v2, pool_w, pool_scale, w_branch_pool, w_branch_nsa, w_out,
           ffn_norm, w_gate_d, w_up_d, w_down_d, w_router, w_gate_e, w_up_e, w_down_e, ple_norm, w_ple_gate,
           w_ple_proj):
    B, T, D = x_prompt.shape
    Bs = x_sample.shape[0]
    depth = w_in.shape[0]
    n_pool, n_pg = cache_k_cmp.shape[1], page_table.shape[1]
    win_buf = state_k_win.shape[2]
    past_len = n_pg * PAGE
    Mp, Ms = B * T, 16
    dff = w_gate_d.shape[-1]
    tm_p, tm_s = 1024, Ms
    tf = 512 if dff % 512 == 0 else dff
    c = lambda a: a.astype(MXU)

    tabs_p, etab_p = _rope_tables(jnp.arange(T))
    tabs_s, etab_s = _rope_tables(jnp.full((Ms,), past_len, jnp.int32))
    ident_pages = jnp.arange(B * (T // PAGE), dtype=jnp.int32).reshape(B, T // PAGE)
    page_table = page_table.astype(jnp.int32)
    n_sel_s = (past_len + 1 + SEL_BLOCK - 1) // SEL_BLOCK

    xp = x_prompt.reshape(Mp, D)
    xs = jnp.pad(x_sample.reshape(Bs, D), ((0, Ms - Bs), (0, 0)))
    pp = p_prompt.reshape(depth, Mp, PLE_DIM)
    ps = jnp.pad(p_sample.reshape(depth, Bs, PLE_DIM), ((0, 0), (0, Ms - Bs), (0, 0)))
    wr_pad = jnp.pad(w_router, ((0, 0), (0, 0), (0, LANES - N_EXPERTS)))
    states_p, states_s = [], []

    for li in range(depth):
        j = li // 2
        lw = dict(attn_norm=attn_norm[li], w_in=_pack_w_in(w_in[li]), q_norm=q_norm[li], ks_norm=ks_norm[li],
                  kw_norm=kw_norm[li], w_branch_pool=c(w_branch_pool[li]), w_branch_nsa=c(w_branch_nsa[li]),
                  w_out=c(w_out[li]), ple_norm=ple_norm[li], w_ple_gate=c(w_ple_gate[li]),
                  w_ple_proj=c(w_ple_proj[li]))
        w1k, w1v = _stack_w1(w_cmp_k1[li]), _stack_w1(w_cmp_v1[li])
        w2k, w2v = c(w_cmp_k2[li]), c(w_cmp_v2[li])
        pw, psc = c(pool_w[li]), pool_scale[li]
        if li % 2 == 0:
            wg, wu, wd = c(w_gate_d[j]), c(w_up_d[j]), c(w_down_d[j])

            def ffn(xp, xs, wg=wg, wu=wu, wd=wd, li=li):
                dense = lambda x, tm: mm_res(ffn_gu(x, ffn_norm[li], wg, wu, tm, tf), wd, x, tm, 512)
                return dense(xp, tm_p), dense(xs, tm_s)
        else:
            wg, wu, wd = c(w_gate_e[j]), c(w_up_e[j]), c(w_down_e[j])

            def ffn(xp, xs, wg=wg, wu=wu, wd=wd, li=li, j=j):
                gate = jnp.concatenate([router(xp, ffn_norm[li], wr_pad[j], 512),
                                        router(xs, ffn_norm[li], wr_pad[j], tm_s)], axis=0)
                row_token, row_scale, tile_expert, n_used, dest = _route(gate)
                xg = gather_rows(jnp.concatenate([xp, xs], axis=0), row_token, GATHER_TG)
                act = moe_gu(xg, ffn_norm[li], wg, wu, tile_expert, n_used)
                yrows = moe_down(act, wd, row_scale, tile_expert, n_used)
                return (gather_add(xp, yrows, dest[:Mp, 0], dest[:Mp, 1], GATHER_TG),
                        gather_add(xs, yrows, dest[Mp:, 0], dest[Mp:, 1], tm_s))

        def attn_p(y, post):
            qn, qr, _, _, ks_aug, kw_b, vs_t, vw_t = post
            a_pool = pool_mix(y.reshape(B, T, N_PACK), OFF_U // POOL_W, pw, psc, 512)
            y3 = y.reshape(Mp // PAGE, PAGE, N_PACK)
            pk = cmp_proj(y3, OFF_KC // KV_W, ident_pages, w1k)
            pv = cmp_proj(y3, OFF_VC // KV_W, ident_pages, w1v)
            kcmp, vcmp = cmp_mlp(pk, pv, w2k, w2v, kc_norm[li])
            o_cmp, selneg = cmp_attn(qn, kcmp, vcmp, T, 256)
            o_slc = slc_attn(qr, selneg, ks_aug, vs_t, T, 256, 512)
            o_nsa = win_attn_combine(qr, kw_b, vw_t, o_cmp, o_slc, y, T, VT_TILE)
            return a_pool, o_nsa

        xp, y, post = _mixer_half(xp, lw, attn_p, tabs_p, etab_p, tm_p, True)
        kv5 = lambda a: a.reshape(B, T, KV_GROUPS, HEAD_DIM)
        ysl = lambda off, w: y[:, off:off + w]
        kw_r, u_p = kv5(post[3]), ysl(OFF_U, POOL_W).reshape(B, T, POOL_W)
        states_p.append((kv5(ysl(OFF_KC, KV_W)), kv5(ysl(OFF_VC, KV_W)), kv5(post[2]), kv5(ysl(OFF_VS, KV_W)),
                         kw_r[:, T - win_buf:], kv5(ysl(OFF_VW, KV_W))[:, T - win_buf:], u_p[:, T - POOL_BUF:]))

        def attn_s(y, post):
            qn, qr, ks_r, kw_r = post[:4]
            u_new = y[:Bs, OFF_U:OFF_U + POOL_W]
            u_ext = jnp.concatenate([state_pool[li], u_new[:, None, :]], axis=1)
            a_pool = pool_mix(u_ext, 0, pw, psc, 16).reshape(Bs, POOL_BUF + 1, POOL_W)[:, POOL_BUF]
            a_pool = jnp.pad(a_pool, ((0, Ms - Bs), (0, 0)))
            pk = cmp_proj_cache(cache_k_cmp, li, page_table, w1k)
            pv = cmp_proj_cache(cache_v_cmp, li, page_table, w1v)
            kcmp, vcmp = cmp_mlp(pk, pv, w2k, w2v, kc_norm[li])
            h4 = lambda a: a[:Bs].reshape(Bs, KV_GROUPS, GQA_R, HEAD_DIM)
            o_cmp4, idx = dec_cmp_attn(h4(qn), kcmp, vcmp, past_len, n_sel_s)
            new = lambda a: a[:Bs].reshape(Bs, 1, KV_W)
            o4 = dec_attn(idx[:, :, 0, :SEL_TOPK].reshape(-1), page_table.reshape(-1), cache_k_slc, cache_v_slc, li,
                          h4(qr), new(ks_r), new(y[:, OFF_VS:OFF_VS + KV_W]), state_k_win, state_v_win,
                          new(kw_r), new(y[:, OFF_VW:OFF_VW + KV_W]), o_cmp4,
                          y[:Bs, OFF_GN:OFF_GN + LANES].reshape(Bs, 1, LANES), past_len)
            return a_pool, jnp.pad(o4.reshape(Bs, Q_W), ((0, Ms - Bs), (0, 0)))

        xs, ys, post_s = _mixer_half(xs, lw, attn_s, tabs_s, etab_s, tm_s, False)
        xp, xs = ffn(xp, xs)
        xp = ple(xp, lw['ple_norm'], lw['w_ple_gate'], pp[li], lw['w_ple_proj'], tm_p, 512)
        xs = ple(xs, lw['ple_norm'], lw['w_ple_gate'], ps[li], lw['w_ple_proj'], tm_s, 512)
        y, post = ys, post_s
        kv5s = lambda a: a[:Bs].reshape(Bs, 1, KV_GROUPS, HEAD_DIM)
        ysl = lambda off, w: y[:, off:off + w]
        k_win = jnp.concatenate([state_k_win[li], kv5s(post[3])], axis=1)[:, -win_buf:]
        v_win = jnp.concatenate([state_v_win[li], kv5s(ysl(OFF_VW, KV_W))], axis=1)[:, -win_buf:]
        pool = jnp.concatenate([state_pool[li], ysl(OFF_U, POOL_W)[:Bs, None, :]], axis=1)[:, -POOL_BUF:]
        states_s.append((kv5s(ysl(OFF_KC, KV_W)), kv5s(ysl(OFF_VC, KV_W)), kv5s(post[2]), kv5s(ysl(OFF_VS, KV_W)),
                         k_win, v_win, pool))

    outs_p = [jnp.stack(a) for a in zip(*states_p)]
    outs_s = [jnp.stack(a) for a in zip(*states_s)]
    return (xp.reshape(B, T, D), xs[:Bs].reshape(Bs, 1, D), *outs_p, *outs_s)
```

```python
import functools

import numpy as np
import jax
import jax.numpy as jnp
from jax import lax
from jax.experimental import pallas as pl
from jax.experimental.pallas import tpu as pltpu

F32 = jnp.float32
MXU = jnp.bfloat16

D_MODEL = 2048
N_HEADS = 16
HEAD_DIM = 128
KV_GROUPS = 4
GQA_R = N_HEADS // KV_GROUPS
ROPE_DIM = HEAD_DIM // 4
ROPE_THETA = 500000.0
SCALE = HEAD_DIM ** -0.5
CMP_LEN = 32
CMP_STRIDE = 16
SEL_BLOCK = 64
SEL_TOPK = 16
WINDOW = 512
POOL_GROUPS = 4
POOL_WINDOWS = (2, 4, 8, 16)
POOL_W = D_MODEL // 2
POOL_GW = POOL_W // POOL_GROUPS
POOL_BUF = max(POOL_WINDOWS) - 1
N_EXPERTS = 8
PLE_DIM = 256
RMS_EPS = 1e-6
Q_W = N_HEADS * HEAD_DIM
KV_W = KV_GROUPS * HEAD_DIM
N_GATES_NSA = 3 * N_HEADS
PAGE = 128

LANES = 128
VMEM_LIMIT_BYTES = 56 * 1024 * 1024
NEG_BIG = -30000.0
NEG_INF = -1e30

OFF_Q, OFF_U, OFF_KC, OFF_VC, OFF_KS, OFF_VS, OFF_KW, OFF_VW = 0, 2048, 3072, 3584, 4096, 4608, 5120, 5632
OFF_GA, OFF_GB, OFF_GN = 6144, 8192, 10240
N_PACK = 10368
PACK_TN = 1152


def _cp(sem):
    return pltpu.CompilerParams(dimension_semantics=sem, vmem_limit_bytes=VMEM_LIMIT_BYTES)


def _rms(x, g):
    ms = jnp.mean(x * x, axis=-1, keepdims=True)
    return x * lax.rsqrt(ms + RMS_EPS) * g


def _dot(a, b):
    return jnp.dot(a, b, preferred_element_type=F32)


def _dot_nt(a, b):
    return lax.dot_general(a, b, (((1,), (1,)), ((), ())), preferred_element_type=F32)


def _dot_f32_by_01(p, sel):
    if MXU == F32:
        return _dot(p, sel)
    hi = p.astype(MXU)
    r1 = p - hi.astype(F32)
    mid = r1.astype(MXU)
    lo = (r1 - mid.astype(F32)).astype(MXU)
    return _dot(hi, sel) + _dot(mid, sel) + _dot(lo, sel)


def _softmax_rows(s, mask):
    s = jnp.where(mask, s, NEG_INF)
    m = jnp.max(s, axis=-1, keepdims=True)
    e = jnp.where(mask, jnp.exp(s - m), 0.0)
    d = jnp.sum(e, axis=-1, keepdims=True)
    return e / jnp.where(d > 0, d, 1.0)


def _norm_mm_body(x_ref, g_ref, w_ref, o_ref, h_ref):
    @pl.when(pl.program_id(1) == 0)
    def _():
        h_ref[...] = _rms(x_ref[...], g_ref[...]).astype(MXU)

    o_ref[...] = _dot(h_ref[...], w_ref[...])


def norm_mm(x, g, w, tm, tn):
    M, K = x.shape
    N = w.shape[1]
    return pl.pallas_call(
        _norm_mm_body, grid=(M // tm, N // tn),
        in_specs=[pl.BlockSpec((tm, K), lambda i, j: (i, 0)),
                  pl.BlockSpec((1, K), lambda i, j: (0, 0)),
                  pl.BlockSpec((K, tn), lambda i, j: (0, j))],
        out_specs=pl.BlockSpec((tm, tn), lambda i, j: (i, j)),
        out_shape=jax.ShapeDtypeStruct((M, N), F32),
        scratch_shapes=[pltpu.VMEM((tm, K), MXU)],
        compiler_params=_cp(("parallel", "arbitrary")), name="norm_mm")(x, g.reshape(1, K), w)


def _mm_res_body(a_ref, w_ref, r_ref, o_ref):
    o_ref[...] = r_ref[...] + _dot(a_ref[...], w_ref[...])


def mm_res(a, w, res, tm, tn):
    M, K = a.shape
    N = res.shape[1]
    return pl.pallas_call(
        _mm_res_body, grid=(M // tm, N // tn),
        in_specs=[pl.BlockSpec((tm, K), lambda i, j: (i, 0)), pl.BlockSpec((K, tn), lambda i, j: (0, j)),
                  pl.BlockSpec((tm, tn), lambda i, j: (i, j))],
        out_specs=pl.BlockSpec((tm, tn), lambda i, j: (i, j)),
        out_shape=jax.ShapeDtypeStruct((M, N), F32),
        compiler_params=_cp(("parallel", "arbitrary")), name="mm_res")(a, w, res)


def _ffn_gu_body(x_ref, g_ref, wg_ref, wu_ref, o_ref, h_ref):
    @pl.when(pl.program_id(1) == 0)
    def _():
        h_ref[...] = _rms(x_ref[...], g_ref[...]).astype(MXU)

    h = h_ref[...]
    a = _dot(h, wg_ref[...])
    b = _dot(h, wu_ref[...])
    o_ref[...] = (a * jax.nn.sigmoid(a) * b).astype(o_ref.dtype)


def ffn_gu(x, g, wg, wu, tm, tf):
    M, K = x.shape
    dff = wg.shape[-1]
    w_spec = pl.BlockSpec((K, tf), lambda i, j: (0, j))
    return pl.pallas_call(
        _ffn_gu_body, grid=(M // tm, dff // tf),
        in_specs=[pl.BlockSpec((tm, K), lambda i, j: (i, 0)),
                  pl.BlockSpec((1, K), lambda i, j: (0, 0)), w_spec, w_spec],
        out_specs=pl.BlockSpec((tm, tf), lambda i, j: (i, j)),
        out_shape=jax.ShapeDtypeStruct((M, dff), MXU),
        scratch_shapes=[pltpu.VMEM((tm, K), MXU)],
        compiler_params=_cp(("parallel", "arbitrary")), name="ffn_gu")(x, g.reshape(1, K), wg, wu)


def _merge_body(ap_ref, on_ref, wp_ref, wn_ref, ga_ref, gb_ref, o_ref):
    yp = _dot(ap_ref[...], wp_ref[...])
    yn = _dot(on_ref[...], wn_ref[...])
    o_ref[...] = (jax.nn.sigmoid(ga_ref[...]) * yp + jax.nn.sigmoid(gb_ref[...]) * yn).astype(o_ref.dtype)


def merge(a_pool, o_nsa, wbp, wbn, y, tm, tn):
    M = a_pool.shape[0]
    ca, cb = OFF_GA // tn, OFF_GB // tn
    return pl.pallas_call(
        _merge_body, grid=(M // tm, D_MODEL // tn),
        in_specs=[pl.BlockSpec((tm, POOL_W), lambda i, j: (i, 0)),
                  pl.BlockSpec((tm, Q_W), lambda i, j: (i, 0)),
                  pl.BlockSpec((POOL_W, tn), lambda i, j: (0, j)),
                  pl.BlockSpec((Q_W, tn), lambda i, j: (0, j)),
                  pl.BlockSpec((tm, tn), lambda i, j: (i, ca + j)),
                  pl.BlockSpec((tm, tn), lambda i, j: (i, cb + j))],
        out_specs=pl.BlockSpec((tm, tn), lambda i, j: (i, j)),
        out_shape=jax.ShapeDtypeStruct((M, D_MODEL), MXU),
        compiler_params=_cp(("parallel", "arbitrary")), name="merge")(a_pool, o_nsa, wbp, wbn, y, y)


def _ple_body(x_ref, xt_ref, g_ref, wg_ref, p_ref, wp_ref, o_ref, h_ref):
    @pl.when(pl.program_id(1) == 0)
    def _():
        h_ref[...] = _rms(x_ref[...], g_ref[...]).astype(MXU)

    gate = jax.nn.sigmoid(_dot(h_ref[...], wg_ref[...]))
    proj = _dot(p_ref[...].astype(MXU), wp_ref[...])
    o_ref[...] = xt_ref[...] + gate * proj


def ple(x, g, wg, p, wp, tm, tn):
    M, K = x.shape
    return pl.pallas_call(
        _ple_body, grid=(M // tm, K // tn),
        in_specs=[pl.BlockSpec((tm, K), lambda i, j: (i, 0)),
                  pl.BlockSpec((tm, tn), lambda i, j: (i, j)),
                  pl.BlockSpec((1, K), lambda i, j: (0, 0)),
                  pl.BlockSpec((K, tn), lambda i, j: (0, j)),
                  pl.BlockSpec((tm, PLE_DIM), lambda i, j: (i, 0)),
                  pl.BlockSpec((PLE_DIM, tn), lambda i, j: (0, j))],
        out_specs=pl.BlockSpec((tm, tn), lambda i, j: (i, j)),
        out_shape=jax.ShapeDtypeStruct((M, K), F32),
        scratch_shapes=[pltpu.VMEM((tm, K), MXU)],
        compiler_params=_cp(("parallel", "arbitrary")), name="ple")(x, x, g.reshape(1, K), wg, p, wp)


def _router_body(x_ref, g_ref, w_ref, o_ref):
    h = _rms(x_ref[...], g_ref[...])
    logits = jnp.dot(h, w_ref[...], preferred_element_type=F32, precision=lax.Precision.HIGHEST)
    lane = lax.broadcasted_iota(jnp.int32, logits.shape, 1)
    logits = jnp.where(lane < N_EXPERTS, logits, NEG_INF)
    m1 = jnp.max(logits, axis=1, keepdims=True)
    i1 = jnp.min(jnp.where(logits == m1, lane, LANES), axis=1, keepdims=True)
    rest = jnp.where(lane == i1, NEG_INF, logits)
    m2 = jnp.max(rest, axis=1, keepdims=True)
    i2 = jnp.min(jnp.where(rest == m2, lane, LANES), axis=1, keepdims=True)
    e2 = jnp.exp(m2 - m1)
    w1 = 1.0 / (1.0 + e2)
    w2 = e2 / (1.0 + e2)
    o_ref[...] = (jnp.where(lane == i1, w1, 0.0) + jnp.where(lane == i2, w2, 0.0)
                  + jnp.where(lane == N_EXPERTS, i1.astype(F32), 0.0)
                  + jnp.where(lane == N_EXPERTS + 1, i2.astype(F32), 0.0))


def router(x, g, wr_pad, tm):
    M, K = x.shape
    return pl.pallas_call(
        _router_body, grid=(M // tm,),
        in_specs=[pl.BlockSpec((tm, K), lambda i: (i, 0)),
                  pl.BlockSpec((1, K), lambda i: (0, 0)),
                  pl.BlockSpec((K, LANES), lambda i: (0, 0))],
        out_specs=pl.BlockSpec((tm, LANES), lambda i: (i, 0)),
        out_shape=jax.ShapeDtypeStruct((M, LANES), F32),
        compiler_params=_cp(("parallel",)), name="router")(x, g.reshape(1, K), wr_pad)


MOE_TM = 512
MOE_TN = 1024
GATHER_TG = 256


def _row_copy(src_ref, row, dst_ref, r, sem):
    return pltpu.make_async_copy(src_ref.at[pl.ds(row, 1)], dst_ref.at[pl.ds(r, 1)], sem)


def _gather_into(idx_ref, base, src_ref, dst_ref, sem, n):
    def issue(r, c):
        _row_copy(src_ref, idx_ref[base + r], dst_ref, r, sem).start()
        return c

    def drain(r, c):
        _row_copy(src_ref, 0, dst_ref, r, sem).wait()
        return c

    lax.fori_loop(0, n, issue, 0, unroll=8)
    lax.fori_loop(0, n, drain, 0, unroll=8)


def _gather_body(idx_ref, src_ref, o_ref, sem, *, tg):
    _gather_into(idx_ref, pl.program_id(0) * tg, src_ref, o_ref, sem, tg)


def gather_rows(src, idx, tg):
    R, D = idx.shape[0], src.shape[1]
    return pl.pallas_call(
        functools.partial(_gather_body, tg=tg),
        grid_spec=pltpu.PrefetchScalarGridSpec(
            num_scalar_prefetch=1, grid=(R // tg,), in_specs=[pl.BlockSpec(memory_space=pl.ANY)],
            out_specs=pl.BlockSpec((tg, D), lambda i, idx: (i, 0)),
            scratch_shapes=[pltpu.SemaphoreType.DMA(())]),
        out_shape=jax.ShapeDtypeStruct((R, D), src.dtype),
        compiler_params=_cp(("arbitrary",)), name="gather_rows")(idx, src)


def _gather_add_body(i1_ref, i2_ref, x_ref, y_ref, o_ref, b1_ref, b2_ref, sem, *, tg):
    base = pl.program_id(0) * tg
    _gather_into(i1_ref, base, y_ref, b1_ref, sem.at[0], tg)
    _gather_into(i2_ref, base, y_ref, b2_ref, sem.at[1], tg)
    o_ref[...] = x_ref[...] + b1_ref[...] + b2_ref[...]


def gather_add(x, y, i1, i2, tg):
    M, D = x.shape
    row = pl.BlockSpec((tg, D), lambda i, a, b: (i, 0))
    return pl.pallas_call(
        functools.partial(_gather_add_body, tg=tg),
        grid_spec=pltpu.PrefetchScalarGridSpec(
            num_scalar_prefetch=2, grid=(M // tg,), in_specs=[row, pl.BlockSpec(memory_space=pl.ANY)],
            out_specs=row,
            scratch_shapes=[pltpu.VMEM((tg, D), F32), pltpu.VMEM((tg, D), F32), pltpu.SemaphoreType.DMA((2,))]),
        out_shape=jax.ShapeDtypeStruct((M, D), F32),
        compiler_params=_cp(("arbitrary",)), name="gather_add")(i1, i2, x, y)


def _moe_gu_body(te_ref, nu_ref, x_ref, g_ref, wg_ref, wu_ref, o_ref):
    del te_ref
    t = pl.program_id(1)

    @pl.when(t < nu_ref[0])
    def _():
        h = _rms(x_ref[...], g_ref[...]).astype(MXU)
        a = _dot(h, wg_ref[...])
        b = _dot(h, wu_ref[...])
        o_ref[...] = (a * jax.nn.sigmoid(a) * b).astype(o_ref.dtype)

    @pl.when(t >= nu_ref[0])
    def _():
        o_ref[...] = jnp.zeros(o_ref.shape, o_ref.dtype)


def moe_gu(xg, g, wg, wu, tile_expert, n_used):
    R, K = xg.shape
    dff = wg.shape[-1]
    tf = dff // 4
    last = lambda t, nu: jnp.minimum(t, nu[0] - 1)
    w_spec = pl.BlockSpec((None, K, tf), lambda f, t, te, nu: (te[last(t, nu)], 0, f))
    return pl.pallas_call(
        _moe_gu_body,
        grid_spec=pltpu.PrefetchScalarGridSpec(
            num_scalar_prefetch=2, grid=(dff // tf, R // MOE_TM),
            in_specs=[pl.BlockSpec((MOE_TM, K), lambda f, t, te, nu: (last(t, nu), 0)),
                      pl.BlockSpec((1, K), lambda f, t, te, nu: (0, 0)), w_spec, w_spec],
            out_specs=pl.BlockSpec((MOE_TM, tf), lambda f, t, te, nu: (t, f))),
        out_shape=jax.ShapeDtypeStruct((R, dff), MXU),
        compiler_params=_cp(("arbitrary", "arbitrary")), name="moe_gu")(
            tile_expert, n_used, xg, g.reshape(1, K), wg, wu)


def _moe_down_body(te_ref, nu_ref, a_ref, w_ref, s_ref, o_ref):
    del te_ref
    t = pl.program_id(1)

    @pl.when(t < nu_ref[0])
    def _():
        o_ref[...] = s_ref[...] * _dot(a_ref[...], w_ref[...])

    @pl.when(t >= nu_ref[0])
    def _():
        o_ref[...] = jnp.zeros(o_ref.shape, o_ref.dtype)


def moe_down(act, wd, row_scale, tile_expert, n_used):
    R, dff = act.shape
    D = wd.shape[-1]
    tn = min(MOE_TN, D)
    last = lambda t, nu: jnp.minimum(t, nu[0] - 1)
    return pl.pallas_call(
        _moe_down_body,
        grid_spec=pltpu.PrefetchScalarGridSpec(
            num_scalar_prefetch=2, grid=(D // tn, R // MOE_TM),
            in_specs=[pl.BlockSpec((MOE_TM, dff), lambda n, t, te, nu: (last(t, nu), 0)),
                      pl.BlockSpec((None, dff, tn), lambda n, t, te, nu: (te[last(t, nu)], 0, n)),
                      pl.BlockSpec((MOE_TM, 1), lambda n, t, te, nu: (last(t, nu), 0))],
            out_specs=pl.BlockSpec((MOE_TM, tn), lambda n, t, te, nu: (t, n))),
        out_shape=jax.ShapeDtypeStruct((R, D), F32),
        compiler_params=_cp(("arbitrary", "arbitrary")), name="moe_down")(
            tile_expert, n_used, act, wd, row_scale)


def _route(gate):
    M = gate.shape[0]
    n_tiles = (2 * M) // MOE_TM + N_EXPERTS
    R = n_tiles * MOE_TM
    e12 = gate[:, N_EXPERTS:N_EXPERTS + 2].astype(jnp.int32)
    w12 = jnp.take_along_axis(gate[:, :N_EXPERTS], e12, axis=1)
    sel = jnp.sum((e12[:, :, None] == jnp.arange(N_EXPERTS)[None, None, :]).astype(jnp.int32), axis=1)
    cnt = jnp.sum(sel, axis=0)
    padded = (cnt + MOE_TM - 1) // MOE_TM * MOE_TM
    ends = jnp.cumsum(padded)
    dest_e = (ends - padded)[None, :] + jnp.cumsum(sel, axis=0) - sel
    dest = jnp.take_along_axis(dest_e, e12, axis=1).astype(jnp.int32)
    tok = jnp.broadcast_to(jnp.arange(M, dtype=jnp.int32)[:, None], (M, 2))
    row_token = jnp.zeros((R,), jnp.int32).at[dest.reshape(-1)].set(tok.reshape(-1))
    row_scale = jnp.zeros((R,), F32).at[dest.reshape(-1)].set(w12.reshape(-1))
    tile_expert = jnp.searchsorted(ends, jnp.arange(n_tiles, dtype=jnp.int32) * MOE_TM, side='right')
    tile_expert = jnp.minimum(tile_expert, N_EXPERTS - 1).astype(jnp.int32)
    n_used = (ends[-1] // MOE_TM).astype(jnp.int32).reshape(1)
    return row_token, row_scale.reshape(R, 1), tile_expert, n_used, dest


def _rope(xh, c, s1, s2):
    return xh * c + pltpu.roll(xh, HEAD_DIM - ROPE_DIM // 2, 1) * s1 + pltpu.roll(xh, ROPE_DIM // 2, 1) * s2


VT_TILE = 256


def _qk_post_body(q_ref, kc_ref, vc_ref, ks_ref, vs_ref, kw_ref, vw_ref, c_ref, s1_ref, s2_ref, e_ref,
                  gq_ref, gs_ref, gw_ref, qn_ref, qr_ref, ksa_ref, kwb_ref,
                  kc4_ref, vc4_ref, ks4_ref, vs4_ref, kw4_ref, vw4_ref, *vt_refs):
    c, s1, s2 = c_ref[...], s1_ref[...], s2_ref[...]
    tm = q_ref.shape[0]
    for h in range(N_HEADS):
        sl = slice(h * HEAD_DIM, (h + 1) * HEAD_DIM)
        qn = _rms(q_ref[:, sl], gq_ref[...])
        qn_ref[:, sl] = (qn * SCALE).astype(qn_ref.dtype)
        qr_ref[:, sl] = (_rope(qn, c, s1, s2) * SCALE).astype(qr_ref.dtype)
    for g in range(KV_GROUPS):
        sl = slice(g * HEAD_DIM, (g + 1) * HEAD_DIM)
        rows4 = pl.ds(g, tm, stride=KV_GROUPS)
        ks = _rope(_rms(ks_ref[:, sl], gs_ref[...]), c, s1, s2)
        kw = _rope(_rms(kw_ref[:, sl], gw_ref[...]), c, s1, s2)
        ks4_ref[rows4, :] = ks
        kw4_ref[rows4, :] = kw
        for src, dst in ((kc_ref, kc4_ref), (vc_ref, vc4_ref), (vs_ref, vs4_ref), (vw_ref, vw4_ref)):
            dst[rows4, :] = src[:, sl]
        ksa_ref[:, 2 * g * HEAD_DIM:(2 * g + 1) * HEAD_DIM] = ks.astype(ksa_ref.dtype)
        ksa_ref[:, (2 * g + 1) * HEAD_DIM:(2 * g + 2) * HEAD_DIM] = e_ref[...]
        kwb_ref[:, sl] = kw.astype(kwb_ref.dtype)
        if vt_refs:
            for v_ref, vt_ref in ((vs_ref, vt_refs[0]), (vw_ref, vt_refs[1])):
                for h in range(vt_ref.shape[1]):
                    vt_ref[g, h] = jnp.transpose(v_ref[h * VT_TILE:(h + 1) * VT_TILE, sl]).astype(vt_ref.dtype)


def qk_post(y, tabs, etab, gq, gs, gw, tm, with_vt):
    M = y.shape[0]
    ntb = tabs[0].shape[0] // tm
    kvb = lambda off: pl.BlockSpec((tm, KV_W), lambda i, off=off: (i, off // KV_W))
    tab = pl.BlockSpec((tm, HEAD_DIM), lambda i: (i % ntb, 0))
    gain = pl.BlockSpec((1, HEAD_DIM), lambda i: (0, 0))
    row = lambda w: pl.BlockSpec((tm, w), lambda i: (i, 0))
    sds = jax.ShapeDtypeStruct
    st_spec = pl.BlockSpec((tm * KV_GROUPS, HEAD_DIM), lambda i: (i, 0))
    out_specs = [row(Q_W), row(Q_W), row(2 * KV_W), row(KV_W)] + [st_spec] * 6
    out_shape = ([sds((M, Q_W), MXU), sds((M, Q_W), MXU), sds((M, 2 * KV_W), MXU), sds((M, KV_W), MXU)]
                 + [sds((M * KV_GROUPS, HEAD_DIM), F32)] * 6)
    if with_vt:
        per = tm // VT_TILE
        vt_spec = pl.BlockSpec((KV_GROUPS, per, HEAD_DIM, VT_TILE), lambda i: (0, i, 0, 0))
        out_specs += [vt_spec, vt_spec]
        out_shape += [sds((KV_GROUPS, M // VT_TILE, HEAD_DIM, VT_TILE), MXU)] * 2
    return pl.pallas_call(
        _qk_post_body, grid=(M // tm,),
        in_specs=[pl.BlockSpec((tm, Q_W), lambda i: (i, OFF_Q // Q_W)), kvb(OFF_KC), kvb(OFF_VC), kvb(OFF_KS),
                  kvb(OFF_VS), kvb(OFF_KW), kvb(OFF_VW), tab, tab, tab, tab, gain, gain, gain],
        out_specs=out_specs, out_shape=out_shape,
        compiler_params=_cp(("parallel",)), name="qk_post")(
            y, y, y, y, y, y, y, tabs[0], tabs[1], tabs[2], etab,
            gq.reshape(1, HEAD_DIM), gs.reshape(1, HEAD_DIM), gw.reshape(1, HEAD_DIM))


def _rope_tables(pos):
    half = ROPE_DIM // 2
    n = pos.shape[0]
    inv = jnp.power(jnp.float32(ROPE_THETA), -jnp.arange(half, dtype=F32) * (2.0 / ROPE_DIM))
    ang = pos.astype(F32)[:, None] * inv[None, :]
    cos, sin = jnp.cos(ang), jnp.sin(ang)
    z = lambda w: jnp.zeros((n, w), F32)
    c = jnp.concatenate([cos, cos, jnp.ones((n, HEAD_DIM - ROPE_DIM), F32)], axis=1)
    s1 = jnp.concatenate([-sin, z(HEAD_DIM - half)], axis=1)
    s2 = jnp.concatenate([z(half), sin, z(HEAD_DIM - ROPE_DIM)], axis=1)
    etab = (pos[:, None] // SEL_BLOCK == jnp.arange(HEAD_DIM)[None, :]).astype(MXU)
    return (c, s1, s2), etab


def _pool_body(u_ref, halo_ref, w_ref, sc_ref, o_ref, *, tp):
    i = pl.program_id(1)
    u = u_ref[...]
    first = (lax.broadcasted_iota(jnp.int32, halo_ref.shape, 0) * 0 + i) == 0
    prev = jnp.where(first, 0.0, halo_ref[...])
    ext = jnp.concatenate([prev, u], axis=0)
    a2 = ext[1:] + ext[:-1]
    a4 = a2[2:, POOL_GW:] + a2[:-2, POOL_GW:]
    a8 = a4[4:, POOL_GW:] + a4[:-4, POOL_GW:]
    a16 = a8[8:, POOL_GW:] + a8[:-8, POOL_GW:]
    sums = (a2[15:, :POOL_GW], a4[13:, :POOL_GW], a8[9:, :POOL_GW], a16[1:])
    t = (i * tp + lax.broadcasted_iota(jnp.int32, (tp, 1), 0) + 1).astype(F32)
    for g, w in enumerate(POOL_WINDOWS):
        sl = slice(g * POOL_GW, (g + 1) * POOL_GW)
        cnt = jnp.minimum(t, float(w))
        z = (sums[g] / cnt - u[:, sl]).astype(MXU)
        o_ref[:, sl] = (_dot(z, w_ref[g]) * sc_ref[:, sl]).astype(o_ref.dtype)


def pool_mix(u3, col, pool_w, pool_scale, tp):
    B, T, _ = u3.shape
    hb = tp // 16
    out = pl.pallas_call(
        functools.partial(_pool_body, tp=tp), grid=(B, T // tp),
        in_specs=[pl.BlockSpec((None, tp, POOL_W), lambda b, i: (b, i, col)),
                  pl.BlockSpec((None, 16, POOL_W), lambda b, i: (b, jnp.maximum(i * hb - 1, 0), col)),
                  pl.BlockSpec((POOL_GROUPS, POOL_GW, POOL_GW), lambda b, i: (0, 0, 0)),
                  pl.BlockSpec((1, POOL_W), lambda b, i: (0, 0))],
        out_specs=pl.BlockSpec((None, tp, POOL_W), lambda b, i: (b, i, 0)),
        out_shape=jax.ShapeDtypeStruct((B, T, POOL_W), MXU),
        compiler_params=_cp(("parallel", "arbitrary")), name="pool_mix")(
            u3, u3, pool_w, pool_scale.reshape(1, POOL_W))
    return out.reshape(B * T, POOL_W)


CMP_PAGES = 32


def _cmp_proj_body(pt_ref, *refs):
    del pt_ref
    pages, w_ref, o_ref = refs[:-2], refs[-2], refs[-1]
    per_page = PAGE // CMP_STRIDE
    cols = []
    for j in range(CMP_STRIDE):
        rows = [pg[pl.ds(j, per_page, stride=CMP_STRIDE), :] for pg in pages]
        cols.append(jnp.concatenate(rows, axis=0).astype(MXU))
    lhs = jnp.concatenate(cols, axis=1)
    o_ref[...] = _dot(lhs, w_ref[...])


def cmp_proj(pages3, col, page_ids, w1s):
    B, n_pg = page_ids.shape
    per_page = PAGE // CMP_STRIDE
    pps = min(CMP_PAGES, n_pg)
    steps = n_pg // pps
    in_specs = [pl.BlockSpec((None, PAGE, HEAD_DIM),
                             lambda b, s, g, pt, k=k: (pt[b * n_pg + s * pps + k], 0, col * KV_GROUPS + g))
                for k in range(pps)]
    in_specs.append(pl.BlockSpec((CMP_STRIDE * HEAD_DIM, 2 * HEAD_DIM), lambda b, s, g, pt: (0, 0)))
    rows = pps * per_page
    return pl.pallas_call(
        _cmp_proj_body,
        grid_spec=pltpu.PrefetchScalarGridSpec(
            num_scalar_prefetch=1, grid=(B, steps, KV_GROUPS), in_specs=in_specs,
            out_specs=pl.BlockSpec((None, rows, 2 * HEAD_DIM), lambda b, s, g, pt: (b, s, g))),
        out_shape=jax.ShapeDtypeStruct((B, n_pg * per_page, KV_GROUPS * 2 * HEAD_DIM), F32),
        compiler_params=_cp(("parallel", "arbitrary", "arbitrary")), name="cmp_proj")(
            page_ids.reshape(-1), *([pages3] * pps), w1s)


def _cmp_proj_cache_body(pt_ref, *refs):
    del pt_ref
    pages, w_ref, o_ref = refs[:-2], refs[-2], refs[-1]
    per_page = PAGE // CMP_STRIDE
    for g in range(KV_GROUPS):
        cols = []
        for j in range(CMP_STRIDE):
            rows = [pg[pl.ds(j * KV_GROUPS + g, per_page, stride=CMP_STRIDE * KV_GROUPS), :] for pg in pages]
            cols.append(jnp.concatenate(rows, axis=0).astype(MXU))
        lhs = jnp.concatenate(cols, axis=1)
        o_ref[:, g * 2 * HEAD_DIM:(g + 1) * 2 * HEAD_DIM] = _dot(lhs, w_ref[...])


def cmp_proj_cache(cache, li, page_table, w1s):
    B, n_pg = page_table.shape
    per_page = PAGE // CMP_STRIDE
    pps = min(CMP_PAGES, n_pg)
    steps = n_pg // pps
    in_specs = [pl.BlockSpec((None, None, PAGE * KV_GROUPS, HEAD_DIM),
                             lambda b, s, pt, k=k: (li, pt[b * n_pg + s * pps + k], 0, 0))
                for k in range(pps)]
    in_specs.append(pl.BlockSpec((CMP_STRIDE * HEAD_DIM, 2 * HEAD_DIM), lambda b, s, pt: (0, 0)))
    rows = pps * per_page
    return pl.pallas_call(
        _cmp_proj_cache_body,
        grid_spec=pltpu.PrefetchScalarGridSpec(
            num_scalar_prefetch=1, grid=(B, steps), in_specs=in_specs,
            out_specs=pl.BlockSpec((None, rows, KV_GROUPS * 2 * HEAD_DIM), lambda b, s, pt: (b, s, 0))),
        out_shape=jax.ShapeDtypeStruct((B, n_pg * per_page, KV_GROUPS * 2 * HEAD_DIM), F32),
        compiler_params=_cp(("parallel", "arbitrary")), name="cmp_proj_cache")(
            page_table.reshape(-1), *([cache] * pps), w1s)


def _cmp_mlp_body(pk_ref, pv_ref, w2k_ref, w2v_ref, g_ref, ko_ref, vo_ref):
    def one(p_ref, w2_ref, g):
        p0 = p_ref[:, 2 * g * HEAD_DIM:(2 * g + 1) * HEAD_DIM]
        p1 = p_ref[:, (2 * g + 1) * HEAD_DIM:(2 * g + 2) * HEAD_DIM]
        pre = p0 + jnp.concatenate([p1[1:], jnp.zeros((1, HEAD_DIM), F32)], axis=0)
        return _dot(jax.nn.gelu(pre).astype(MXU), w2_ref[...])

    for g in range(KV_GROUPS):
        ko_ref[g] = _rms(one(pk_ref, w2k_ref, g), g_ref[...]).astype(ko_ref.dtype)
        vo_ref[g] = one(pv_ref, w2v_ref, g).astype(vo_ref.dtype)


def cmp_mlp(pk, pv, w2k, w2v, kc_g):
    B, n, w = pk.shape
    p_spec = pl.BlockSpec((None, n, w), lambda b: (b, 0, 0))
    w_spec = pl.BlockSpec((HEAD_DIM, HEAD_DIM), lambda b: (0, 0))
    o_spec = pl.BlockSpec((None, KV_GROUPS, n, HEAD_DIM), lambda b: (b, 0, 0, 0))
    sd = jax.ShapeDtypeStruct((B, KV_GROUPS, n, HEAD_DIM), MXU)
    return pl.pallas_call(
        _cmp_mlp_body, grid=(B,),
        in_specs=[p_spec, p_spec, w_spec, w_spec, pl.BlockSpec((1, HEAD_DIM), lambda b: (0, 0))],
        out_specs=[o_spec, o_spec], out_shape=[sd, sd],
        compiler_params=_cp(("parallel",)), name="cmp_mlp")(pk, pv, w2k, w2v, kc_g.reshape(1, HEAD_DIM))


def _sel_matrix(n_cmp_rows, n_lanes):
    per = SEL_BLOCK // CMP_STRIDE
    c = np.arange(n_cmp_rows)[:, None]
    j = np.arange(n_lanes)[None, :]
    lo = per * j - (CMP_LEN // CMP_STRIDE - 1)
    return jnp.asarray((c >= lo) & (c <= per * j + per - 1), dtype=MXU)


def _stack_heads(x):
    return jnp.concatenate([x[:, r * HEAD_DIM:(r + 1) * HEAD_DIM] for r in range(GQA_R)], axis=0)


def _cmp_attn_body(q_ref, k_ref, v_ref, sel_ref, o_ref, sn_ref, *, tq, n_sel):
    i = pl.program_id(2)
    qs = _stack_heads(q_ref[...])
    s = _dot_nt(qs, k_ref[...])
    nc = s.shape[1]
    t_rows = i * tq + (lax.broadcasted_iota(jnp.int32, s.shape, 0) & (tq - 1))
    cmp_end = lax.broadcasted_iota(jnp.int32, s.shape, 1) * CMP_STRIDE + (CMP_LEN - 1)
    p = _softmax_rows(s, cmp_end <= t_rows)
    o = _dot(p.astype(MXU), v_ref[...])
    for r in range(GQA_R):
        o_ref[:, r * HEAD_DIM:(r + 1) * HEAD_DIM] = o[r * tq:(r + 1) * tq]
    p_sum = p[0:tq] + p[tq:2 * tq] + p[2 * tq:3 * tq] + p[3 * tq:4 * tq]
    imp = _dot_f32_by_01(p_sum, sel_ref[...])
    t = i * tq + lax.broadcasted_iota(jnp.int32, imp.shape, 0)
    blk = lax.broadcasted_iota(jnp.int32, imp.shape, 1)
    cur = t // SEL_BLOCK
    forced = (blk == 0) | (blk == cur) | (blk == cur - 1)
    score = jnp.where(forced, jnp.inf, jnp.where(blk * SEL_BLOCK <= t, imp, -jnp.inf))
    st = jnp.transpose(score)[:n_sel]
    nb = n_sel // 8
    tiles = [st[8 * a:8 * a + 8] for a in range(nb)]
    ranks = [jnp.zeros((8, tq), F32) for _ in range(nb)]
    sub = lax.broadcasted_iota(jnp.int32, (8, tq), 0)
    for b in range(n_sel):
        row = jnp.broadcast_to(st[b:b + 1], (8, tq))
        for a in range(nb):
            if 8 * a > b:
                beats = jnp.where(row >= tiles[a], 1.0, 0.0)
            elif 8 * a + 7 < b:
                beats = jnp.where(row > tiles[a], 1.0, 0.0)
            else:
                beats = jnp.where(sub + 8 * a > b, jnp.where(row >= tiles[a], 1.0, 0.0),
                                  jnp.where(row > tiles[a], 1.0, 0.0))
            ranks[a] = ranks[a] + beats
    neg = [jnp.where(r < float(SEL_TOPK), 0.0, NEG_BIG) for r in ranks]
    neg_t = jnp.concatenate(neg + [jnp.zeros((LANES - n_sel, tq), F32)], axis=0)
    sn_ref[...] = jnp.transpose(neg_t).astype(sn_ref.dtype)


def cmp_attn(qn, kcmp, vcmp, T, tq):
    M = qn.shape[0]
    B = M // T
    nc = kcmp.shape[2]
    n_sel = T // SEL_BLOCK
    nt = T // tq
    sel = _sel_matrix(nc, LANES)
    kv_spec = pl.BlockSpec((None, None, nc, HEAD_DIM), lambda b, g, i: (b, g, 0, 0))
    return pl.pallas_call(
        functools.partial(_cmp_attn_body, tq=tq, n_sel=n_sel), grid=(B, KV_GROUPS, nt),
        in_specs=[pl.BlockSpec((tq, GQA_R * HEAD_DIM), lambda b, g, i: (b * nt + i, g)), kv_spec, kv_spec,
                  pl.BlockSpec((nc, LANES), lambda b, g, i: (0, 0))],
        out_specs=[pl.BlockSpec((tq, GQA_R * HEAD_DIM), lambda b, g, i: (b * nt + i, g)),
                   pl.BlockSpec((None, tq, LANES), lambda b, g, i: (g, b * nt + i, 0))],
        out_shape=[jax.ShapeDtypeStruct((M, Q_W), F32), jax.ShapeDtypeStruct((KV_GROUPS, M, LANES), MXU)],
        compiler_params=_cp(("parallel", "parallel", "arbitrary")), name="cmp_attn")(qn, kcmp, vcmp, sel)


def _slc_body(q_ref, sn_ref, k_ref, vt_ref, o_ref, qa_ref, st_a, st_b, pb_a, pb_b, al_a, al_b, m_ref, l_ref,
              acc_ref, *, tq, tk):
    i = pl.program_id(2)
    sn = sn_ref[...]
    for r in range(GQA_R):
        qa_ref[r * tq:(r + 1) * tq, :HEAD_DIM] = q_ref[:, r * HEAD_DIM:(r + 1) * HEAD_DIM]
        qa_ref[r * tq:(r + 1) * tq, HEAD_DIM:] = sn
    m_ref[...] = jnp.full(m_ref.shape, NEG_INF, F32)
    l_ref[...] = jnp.zeros(l_ref.shape, F32)
    acc_ref[...] = jnp.zeros(acc_ref.shape, F32)
    pb_b[...] = jnp.zeros(pb_b.shape, pb_b.dtype)
    al_b[...] = jnp.ones(al_b.shape, F32)
    last = (i * tq + tq - 1) // tk
    per = tk // VT_TILE

    def scores(j, st_ref):
        k = k_ref[pl.ds(pl.multiple_of(j * tk, tk), tk), :]
        st_ref[...] = _dot_nt(k, qa_ref[...])

    def softmax(j, st_ref, pb_ref, al_ref, causal):
        st = st_ref[...]
        if causal:
            kpos = j * tk + lax.broadcasted_iota(jnp.int32, st.shape, 0)
            t_cols = i * tq + (lax.broadcasted_iota(jnp.int32, st.shape, 1) & (tq - 1))
            st = jnp.where(kpos <= t_cols, st, NEG_INF)
        m_old = m_ref[...]
        m_new = jnp.maximum(m_old, jnp.max(st, axis=0, keepdims=True))
        alpha = jnp.exp(m_old - m_new)
        p = jnp.exp(st - m_new)
        l_ref[...] = alpha * l_ref[...] + jnp.sum(p, axis=0, keepdims=True)
        pb_ref[...] = p.astype(MXU)
        al_ref[...] = alpha
        m_ref[...] = m_new

    def accumulate(j, pb_ref, al_ref):
        j = jnp.maximum(j, 0)
        pv = _dot(vt_ref[j * per], pb_ref[:VT_TILE])
        for h in range(1, per):
            pv = pv + _dot(vt_ref[j * per + h], pb_ref[h * VT_TILE:(h + 1) * VT_TILE])
        acc_ref[...] = al_ref[...] * acc_ref[...] + pv

    scores(0, st_a)
    n_pairs = last // 2

    def pair(p, carry):
        j = 2 * p
        softmax(j, st_a, pb_a, al_a, False)
        scores(j + 1, st_b)
        accumulate(j - 1, pb_b, al_b)
        softmax(j + 1, st_b, pb_b, al_b, False)
        scores(j + 2, st_a)
        accumulate(j, pb_a, al_a)
        return carry

    lax.fori_loop(0, n_pairs, pair, 0)
    j = 2 * n_pairs

    @pl.when(last == j)
    def _():
        softmax(j, st_a, pb_a, al_a, True)
        accumulate(j - 1, pb_b, al_b)
        accumulate(j, pb_a, al_a)

    @pl.when(last != j)
    def _():
        softmax(j, st_a, pb_a, al_a, False)
        scores(j + 1, st_b)
        accumulate(j - 1, pb_b, al_b)
        softmax(j + 1, st_b, pb_b, al_b, True)
        accumulate(j, pb_a, al_a)
        accumulate(j + 1, pb_b, al_b)

    o = acc_ref[...] / l_ref[...]
    for r in range(GQA_R):
        o_ref[:, r * HEAD_DIM:(r + 1) * HEAD_DIM] = jnp.transpose(o[:, r * tq:(r + 1) * tq])


def slc_attn(qr, selneg, ks_aug, vs_t, T, tq, tk):
    M = qr.shape[0]
    B = M // T
    nt = T // tq
    per_b = T // VT_TILE
    return pl.pallas_call(
        functools.partial(_slc_body, tq=tq, tk=tk), grid=(B, KV_GROUPS, nt),
        in_specs=[pl.BlockSpec((tq, GQA_R * HEAD_DIM), lambda b, g, i: (b * nt + i, g)),
                  pl.BlockSpec((None, tq, LANES), lambda b, g, i: (g, b * nt + i, 0)),
                  pl.BlockSpec((T, 2 * HEAD_DIM), lambda b, g, i: (b, g)),
                  pl.BlockSpec((None, per_b, HEAD_DIM, VT_TILE), lambda b, g, i: (g, b, 0, 0))],
        out_specs=pl.BlockSpec((tq, GQA_R * HEAD_DIM), lambda b, g, i: (b * nt + i, g)),
        out_shape=jax.ShapeDtypeStruct((M, Q_W), F32),
        scratch_shapes=[pltpu.VMEM((GQA_R * tq, 2 * HEAD_DIM), MXU),
                        pltpu.VMEM((tk, GQA_R * tq), F32), pltpu.VMEM((tk, GQA_R * tq), F32),
                        pltpu.VMEM((tk, GQA_R * tq), MXU), pltpu.VMEM((tk, GQA_R * tq), MXU),
                        pltpu.VMEM((1, GQA_R * tq), F32), pltpu.VMEM((1, GQA_R * tq), F32),
                        pltpu.VMEM((1, GQA_R * tq), F32), pltpu.VMEM((1, GQA_R * tq), F32),
                        pltpu.VMEM((HEAD_DIM, GQA_R * tq), F32)],
        compiler_params=_cp(("parallel", "parallel", "arbitrary")), name="slc_attn")(
            qr, selneg, ks_aug, vs_t)


def _lane_col(x, c):
    lane = lax.broadcasted_iota(jnp.int32, x.shape, 1)
    return jnp.sum(jnp.where(lane == c, x, 0.0), axis=1, keepdims=True)


def _win_body(q_ref, k0, k1, k2, vt0, vt1, vt2, oc_ref, os_ref, gn_ref, o_ref, *, tq):
    g, i = pl.program_id(1), pl.program_id(2)
    qs = _stack_heads(q_ref[...])
    k = jnp.concatenate([k0[...], k1[...], k2[...]], axis=0)
    st = _dot_nt(k, qs)
    kp = (i - 2) * tq + lax.broadcasted_iota(jnp.int32, st.shape, 0)
    qp = i * tq + (lax.broadcasted_iota(jnp.int32, st.shape, 1) & (tq - 1))
    st = jnp.where((kp <= qp) & (kp > qp - WINDOW) & (kp >= 0), st, NEG_INF)
    m = jnp.max(st, axis=0, keepdims=True)
    e = jnp.exp(st - m)
    l = jnp.sum(e, axis=0, keepdims=True)
    eb = e.astype(MXU)
    ot = (_dot(vt0[...], eb[:tq]) + _dot(vt1[...], eb[tq:2 * tq]) + _dot(vt2[...], eb[2 * tq:])) / l
    gates = jax.nn.sigmoid(gn_ref[...])
    for r in range(GQA_R):
        sl = slice(r * HEAD_DIM, (r + 1) * HEAD_DIM)
        h = g * GQA_R + r
        ow = jnp.transpose(ot[:, r * tq:(r + 1) * tq])
        o = (_lane_col(gates, h) * oc_ref[:, sl] + _lane_col(gates, N_HEADS + h) * os_ref[:, sl]
             + _lane_col(gates, 2 * N_HEADS + h) * ow)
        o_ref[:, sl] = o.astype(o_ref.dtype)


def win_attn_combine(qr, kw_b, vw_t, o_cmp, o_slc, y, T, tq):
    assert WINDOW == 2 * tq and tq == VT_TILE
    M = qr.shape[0]
    B = M // T
    nt = T // tq
    qspec = pl.BlockSpec((tq, GQA_R * HEAD_DIM), lambda b, g, i: (b * nt + i, g))
    kv = [pl.BlockSpec((tq, HEAD_DIM), lambda b, g, i, o=o: (b * nt + jnp.maximum(i - 2 + o, 0), g)) for o in range(3)]
    vt = [pl.BlockSpec((None, None, HEAD_DIM, tq), lambda b, g, i, o=o: (g, b * nt + jnp.maximum(i - 2 + o, 0), 0, 0))
          for o in range(3)]
    return pl.pallas_call(
        functools.partial(_win_body, tq=tq), grid=(B, KV_GROUPS, nt),
        in_specs=[qspec] + kv + vt + [qspec, qspec,
                                      pl.BlockSpec((tq, LANES), lambda b, g, i: (b * nt + i, OFF_GN // LANES))],
        out_specs=qspec, out_shape=jax.ShapeDtypeStruct((M, Q_W), MXU),
        compiler_params=_cp(("parallel", "parallel", "arbitrary")), name="win_attn")(
            qr, kw_b, kw_b, kw_b, vw_t, vw_t, vw_t, o_cmp, o_slc, y)


def _dec_cmp_body(q_ref, k_ref, v_ref, sel_ref, o_ref, idx_ref, *, qpos, n_sel):
    p_sums = []
    for g in range(KV_GROUPS):
        s = _dot_nt(q_ref[g], k_ref[g])
        cmp_end = lax.broadcasted_iota(jnp.int32, s.shape, 1) * CMP_STRIDE + (CMP_LEN - 1)
        p = _softmax_rows(s, cmp_end <= qpos)
        o_ref[g] = _dot(p.astype(MXU), v_ref[g])
        p_sums.append(jnp.sum(p, axis=0, keepdims=True))
    imp = _dot_f32_by_01(jnp.concatenate(p_sums, axis=0), sel_ref[...])
    blk = lax.broadcasted_iota(jnp.int32, imp.shape, 1)
    cur = qpos // SEL_BLOCK
    forced = (blk == 0) | (blk == cur) | (blk == cur - 1)
    valid = blk < n_sel
    score = jnp.where(forced, jnp.inf, jnp.where(blk * SEL_BLOCK <= qpos, imp, -jnp.inf))
    lane = lax.broadcasted_iota(jnp.int32, (KV_GROUPS, LANES), 1)
    out = jnp.zeros((KV_GROUPS, LANES), jnp.int32)
    taken = ~valid
    big = imp.shape[1]
    for k in range(SEL_TOPK):
        cand = jnp.where(taken, -jnp.inf, score)
        m = jnp.max(cand, axis=1, keepdims=True)
        pick = jnp.min(jnp.where((cand == m) & ~taken, blk, big), axis=1, keepdims=True)
        out = jnp.where(lane == k, pick, out)
        taken = taken | (blk == pick)
    for g in range(KV_GROUPS):
        idx_ref[g] = out[g:g + 1]


def dec_cmp_attn(qn4, kcmp, vcmp, qpos, n_sel):
    B = qn4.shape[0]
    nc = kcmp.shape[2]
    lanes = -(-n_sel // LANES) * LANES
    sel = _sel_matrix(nc, lanes)
    q_spec = pl.BlockSpec((None, KV_GROUPS, GQA_R, HEAD_DIM), lambda b: (b, 0, 0, 0))
    kv_spec = pl.BlockSpec((None, KV_GROUPS, nc, HEAD_DIM), lambda b: (b, 0, 0, 0))
    return pl.pallas_call(
        functools.partial(_dec_cmp_body, qpos=qpos, n_sel=n_sel), grid=(B,),
        in_specs=[q_spec, kv_spec, kv_spec, pl.BlockSpec((nc, lanes), lambda b: (0, 0))],
        out_specs=[q_spec, pl.BlockSpec((None, KV_GROUPS, 1, LANES), lambda b: (b, 0, 0, 0))],
        out_shape=[jax.ShapeDtypeStruct((B, KV_GROUPS, GQA_R, HEAD_DIM), F32),
                   jax.ShapeDtypeStruct((B, KV_GROUPS, 1, LANES), jnp.int32)],
        compiler_params=_cp(("parallel",)), name="dec_cmp_attn")(qn4, kcmp, vcmp, sel)


def _pick_group(ref, g):
    return ref[pl.ds(g, ref.shape[0] // KV_GROUPS, stride=KV_GROUPS), :]


def _dec_attn_body(idx_ref, pt_ref, *refs, qpos, n_past_blocks):
    del pt_ref
    K = SEL_TOPK
    kb, vb = refs[:K], refs[K:2 * K]
    (q_ref, kn_ref, vn_ref, kwp_ref, vwp_ref, kwn_ref, vwn_ref, oc_ref, gn_ref, o_ref) = refs[2 * K:]
    b, g = pl.program_id(0), pl.program_id(1)
    q = q_ref[...]
    ks = jnp.concatenate([_pick_group(r, g).astype(MXU) for r in kb], axis=0)
    vs = jnp.concatenate([_pick_group(r, g).astype(MXU) for r in vb], axis=0)
    s = _dot_nt(q, ks)
    col = lax.broadcasted_iota(jnp.int32, s.shape, 1)
    blk_of_col = jnp.zeros(s.shape, jnp.int32)
    for k in range(K):
        blk_of_col = jnp.where(col // SEL_BLOCK == k, idx_ref[(b * KV_GROUPS + g) * K + k], blk_of_col)
    mask = (blk_of_col < n_past_blocks) & (blk_of_col * SEL_BLOCK + (col & (SEL_BLOCK - 1)) <= qpos)
    s = jnp.where(mask, s, NEG_INF)
    has_new = jnp.max(jnp.where(blk_of_col == qpos // SEL_BLOCK, 1.0, 0.0), axis=1, keepdims=True) > 0.5
    kn = kn_ref[...]
    s_new = jnp.where(has_new, jnp.sum(q.astype(F32) * kn.astype(MXU).astype(F32), axis=1, keepdims=True), NEG_INF)
    m = jnp.maximum(jnp.max(s, axis=1, keepdims=True), s_new)
    e = jnp.where(mask, jnp.exp(s - m), 0.0)
    e_new = jnp.where(has_new, jnp.exp(s_new - m), 0.0)
    d = jnp.sum(e, axis=1, keepdims=True) + e_new
    d = jnp.where(d > 0, d, 1.0)
    o_slc = (_dot(e.astype(MXU), vs) + e_new.astype(MXU).astype(F32) * vn_ref[...].astype(MXU).astype(F32)) / d
    kw = _pick_group(kwp_ref, g).astype(MXU)
    n_buf = kw.shape[0]
    sw = _dot_nt(q, kw)
    kp = qpos - n_buf + lax.broadcasted_iota(jnp.int32, sw.shape, 1)
    wmask = (kp > qpos - WINDOW) & (kp >= 0)
    sw = jnp.where(wmask, sw, NEG_INF)
    sw_new = jnp.sum(q.astype(F32) * kwn_ref[...].astype(MXU).astype(F32), axis=1, keepdims=True)
    mw = jnp.maximum(jnp.max(sw, axis=1, keepdims=True), sw_new)
    ew = jnp.where(wmask, jnp.exp(sw - mw), 0.0)
    ew_new = jnp.exp(sw_new - mw)
    dw = jnp.sum(ew, axis=1, keepdims=True) + ew_new
    o_win = (_dot(ew.astype(MXU), _pick_group(vwp_ref, g).astype(MXU))
             + ew_new.astype(MXU).astype(F32) * vwn_ref[...].astype(MXU).astype(F32)) / dw
    gates = jax.nn.sigmoid(gn_ref[...])
    rows = lax.broadcasted_iota(jnp.int32, (GQA_R, LANES), 0)
    lane = lax.broadcasted_iota(jnp.int32, (GQA_R, LANES), 1)
    gb = jnp.broadcast_to(gates, (GQA_R, LANES))
    pick = lambda base: jnp.sum(jnp.where(lane == base + g * GQA_R + rows, gb, 0.0), axis=1, keepdims=True)
    o = pick(0) * oc_ref[...] + pick(N_HEADS) * o_slc + pick(2 * N_HEADS) * o_win
    o_ref[...] = o.astype(o_ref.dtype)


def dec_attn(idx, page_ids, ks_cache, vs_cache, li, qr4, ks_new, vs_new, kw_past, vw_past, kw_new, vw_new,
             o_cmp4, gn3, qpos):
    B = qr4.shape[0]
    n_pg = page_ids.shape[0] // B
    n_past_blocks = n_pg * (PAGE // SEL_BLOCK)
    buf_rows = kw_past.shape[2]
    per_page = PAGE // SEL_BLOCK

    def blk_spec(k):
        def im(b, g, idx_ref, pt_ref, k=k):
            blk = jnp.minimum(idx_ref[(b * KV_GROUPS + g) * SEL_TOPK + k], n_past_blocks - 1)
            return (li, pt_ref[b * n_pg + blk // per_page], blk % per_page, 0)
        return pl.BlockSpec((None, None, SEL_BLOCK * KV_GROUPS, HEAD_DIM), im)

    q_spec = pl.BlockSpec((None, None, GQA_R, HEAD_DIM), lambda b, g, i_, p_: (b, g, 0, 0))
    new_spec = pl.BlockSpec((None, 1, HEAD_DIM), lambda b, g, i_, p_: (b, 0, g))
    past_spec = pl.BlockSpec((None, None, buf_rows, HEAD_DIM), lambda b, g, i_, p_: (li, b, 0, 0))
    in_specs = ([blk_spec(k) for k in range(SEL_TOPK)] + [blk_spec(k) for k in range(SEL_TOPK)]
                + [q_spec, new_spec, new_spec, past_spec, past_spec, new_spec, new_spec, q_spec,
                   pl.BlockSpec((None, 1, LANES), lambda b, g, i_, p_: (b, 0, 0))])
    return pl.pallas_call(
        functools.partial(_dec_attn_body, qpos=qpos, n_past_blocks=n_past_blocks),
        grid_spec=pltpu.PrefetchScalarGridSpec(
            num_scalar_prefetch=2, grid=(B, KV_GROUPS), in_specs=in_specs, out_specs=q_spec),
        out_shape=jax.ShapeDtypeStruct((B, KV_GROUPS, GQA_R, HEAD_DIM), MXU),
        compiler_params=_cp(("parallel", "arbitrary")), name="dec_attn")(
            idx, page_ids, *([ks_cache] * SEL_TOPK), *([vs_cache] * SEL_TOPK),
            qr4, ks_new, vs_new, kw_past, vw_past, kw_new, vw_new, o_cmp4, gn3)


def _pack_w_in(w):
    d = w.shape[0]
    u, q, kv, gn, ga, gb = (w[:, :POOL_W], w[:, POOL_W:POOL_W + Q_W], w[:, POOL_W + Q_W:POOL_W + Q_W + 6 * KV_W],
                            w[:, 6144:6144 + N_GATES_NSA], w[:, 6192:6192 + D_MODEL], w[:, 8240:8240 + D_MODEL])
    pad = jnp.zeros((d, LANES - N_GATES_NSA), w.dtype)
    return jnp.concatenate([q, u, kv, ga, gb, gn, pad], axis=1).astype(MXU)


def _stack_w1(w1):
    r = CMP_LEN // CMP_STRIDE
    w = w1.reshape(r, CMP_STRIDE * HEAD_DIM, HEAD_DIM)
    return jnp.concatenate([w[o] for o in range(r)], axis=1).astype(MXU)


def _mixer_half(x, lw, attn, tabs, etab, tm, with_vt):
    y = norm_mm(x, lw['attn_norm'], lw['w_in'], tm, PACK_TN)
    post = qk_post(y, tabs, etab, lw['q_norm'], lw['ks_norm'], lw['kw_norm'], min(tm, 512), with_vt)
    a_pool, o_nsa = attn(y, post)
    merged = merge(a_pool, o_nsa, lw['w_branch_pool'], lw['w_branch_nsa'], y, tm, 512)
    return mm_res(merged, lw['w_out'], x, tm, 512), y, post


def kernel(x_prompt, x_sample, cache_k_cmp, cache_v_cmp, cache_k_slc, cache_v_slc, state_k_win, state_v_win,
           state_pool, page_table, p_prompt, p_sample, attn_norm, w_in, q_norm, kc_norm, ks_norm, kw_norm,
           w_cmp_k1, w_cmp_k2, w_cmp_v1, w_cmp_v2, pool_w, pool_scale, w_branch_pool, w_branch_nsa, w_out,
           ffn_norm, w_gate_d, w_up_d, w_down_d, w_router, w_gate_e, w_up_e, w_down_e, ple_norm, w_ple_gate,
           w_ple_proj):
    B, T, D = x_prompt.shape
    Bs = x_sample.shape[0]
    depth = w_in.shape[0]
    n_pool, n_pg = cache_k_cmp.shape[1], page_table.shape[1]
    win_buf = state_k_win.shape[2]
    past_len = n_pg * PAGE
    Mp, Ms = B * T, 16
    dff = w_gate_d.shape[-1]
    tm_p, tm_s = 1024, Ms
    tf = 512 if dff % 512 == 0 else dff
    c = lambda a: a.astype(MXU)

    tabs_p, etab_p = _rope_tables(jnp.arange(T))
    tabs_s, etab_s = _rope_tables(jnp.full((Ms,), past_len, jnp.int32))
    ident_pages = jnp.arange(B * (T // PAGE), dtype=jnp.int32).reshape(B, T // PAGE)
    page_table = page_table.astype(jnp.int32)
    rows4 = lambda a: a.reshape(a.shape[:-3] + (a.shape[-3] * KV_GROUPS, HEAD_DIM))
    n_sel_s = (past_len + 1 + SEL_BLOCK - 1) // SEL_BLOCK

    xp = x_prompt.reshape(Mp, D)
    xs = jnp.pad(x_sample.reshape(Bs, D), ((0, Ms - Bs), (0, 0)))
    pp = p_prompt.reshape(depth, Mp, PLE_DIM)
    ps = jnp.pad(p_sample.reshape(depth, Bs, PLE_DIM), ((0, 0), (0, Ms - Bs), (0, 0)))
    wr_pad = jnp.pad(w_router, ((0, 0), (0, 0), (0, LANES - N_EXPERTS)))
    states_p, states_s = [], []

    for li in range(depth):
        j = li // 2
        lw = dict(attn_norm=attn_norm[li], w_in=_pack_w_in(w_in[li]), q_norm=q_norm[li], ks_norm=ks_norm[li],
                  kw_norm=kw_norm[li], w_branch_pool=c(w_branch_pool[li]), w_branch_nsa=c(w_branch_nsa[li]),
                  w_out=c(w_out[li]), ple_norm=ple_norm[li], w_ple_gate=c(w_ple_gate[li]),
                  w_ple_proj=c(w_ple_proj[li]))
        w1k, w1v = _stack_w1(w_cmp_k1[li]), _stack_w1(w_cmp_v1[li])
        w2k, w2v = c(w_cmp_k2[li]), c(w_cmp_v2[li])
        pw, psc = c(pool_w[li]), pool_scale[li]
        if li % 2 == 0:
            wg, wu, wd = c(w_gate_d[j]), c(w_up_d[j]), c(w_down_d[j])

            def ffn(xp, xs, wg=wg, wu=wu, wd=wd, li=li):
                dense = lambda x, tm: mm_res(ffn_gu(x, ffn_norm[li], wg, wu, tm, tf), wd, x, tm, 512)
                return dense(xp, tm_p), dense(xs, tm_s)
        else:
            wg, wu, wd = c(w_gate_e[j]), c(w_up_e[j]), c(w_down_e[j])

            def ffn(xp, xs, wg=wg, wu=wu, wd=wd, li=li, j=j):
                gate = jnp.concatenate([router(xp, ffn_norm[li], wr_pad[j], 512),
                                        router(xs, ffn_norm[li], wr_pad[j], tm_s)], axis=0)
                row_token, row_scale, tile_expert, n_used, dest = _route(gate)
                xg = gather_rows(jnp.concatenate([xp, xs], axis=0), row_token, GATHER_TG)
                act = moe_gu(xg, ffn_norm[li], wg, wu, tile_expert, n_used)
                yrows = moe_down(act, wd, row_scale, tile_expert, n_used)
                return (gather_add(xp, yrows, dest[:Mp, 0], dest[:Mp, 1], GATHER_TG),
                        gather_add(xs, yrows, dest[Mp:, 0], dest[Mp:, 1], tm_s))

        def attn_p(y, post):
            qn, qr, ks_aug, kw_b = post[:4]
            vs_t, vw_t = post[10:]
            a_pool = pool_mix(y.reshape(B, T, N_PACK), OFF_U // POOL_W, pw, psc, 512)
            y3 = y.reshape(Mp // PAGE, PAGE, N_PACK)
            pk = cmp_proj(y3, OFF_KC // KV_W, ident_pages, w1k)
            pv = cmp_proj(y3, OFF_VC // KV_W, ident_pages, w1v)
            kcmp, vcmp = cmp_mlp(pk, pv, w2k, w2v, kc_norm[li])
            o_cmp, selneg = cmp_attn(qn, kcmp, vcmp, T, 256)
            o_slc = slc_attn(qr, selneg, ks_aug, vs_t, T, 256, 512)
            o_nsa = win_attn_combine(qr, kw_b, vw_t, o_cmp, o_slc, y, T, VT_TILE)
            return a_pool, o_nsa

        xp, y, post = _mixer_half(xp, lw, attn_p, tabs_p, etab_p, tm_p, True)
        kc4, vc4, ks4, vs4, kw4, vw4 = (a.reshape(B, T, KV_GROUPS, HEAD_DIM) for a in post[4:10])
        u_p = y[:, OFF_U:OFF_U + POOL_W].reshape(B, T, POOL_W)
        states_p.append((kc4, vc4, ks4, vs4, kw4[:, T - win_buf:], vw4[:, T - win_buf:], u_p[:, T - POOL_BUF:]))

        def attn_s(y, post):
            qn, qr = post[:2]
            ks_r, vs_r, kw_r, vw_r = post[6:10]
            u_new = y[:Bs, OFF_U:OFF_U + POOL_W]
            u_ext = jnp.concatenate([state_pool[li], u_new[:, None, :]], axis=1)
            a_pool = pool_mix(u_ext, 0, pw, psc, 16).reshape(Bs, POOL_BUF + 1, POOL_W)[:, POOL_BUF]
            a_pool = jnp.pad(a_pool, ((0, Ms - Bs), (0, 0)))
            pk = cmp_proj_cache(rows4(cache_k_cmp), li, page_table, w1k)
            pv = cmp_proj_cache(rows4(cache_v_cmp), li, page_table, w1v)
            kcmp, vcmp = cmp_mlp(pk, pv, w2k, w2v, kc_norm[li])
            h4 = lambda a: a[:Bs].reshape(Bs, KV_GROUPS, GQA_R, HEAD_DIM)
            o_cmp4, idx = dec_cmp_attn(h4(qn), kcmp, vcmp, past_len, n_sel_s)
            new = lambda a: a[:Bs * KV_GROUPS].reshape(Bs, 1, KV_W)
            o4 = dec_attn(idx[:, :, 0, :SEL_TOPK].reshape(-1), page_table.reshape(-1), rows4(cache_k_slc),
                          rows4(cache_v_slc), li, h4(qr), new(ks_r), new(vs_r), rows4(state_k_win),
                          rows4(state_v_win), new(kw_r), new(vw_r), o_cmp4,
                          y[:Bs, OFF_GN:OFF_GN + LANES].reshape(Bs, 1, LANES), past_len)
            return a_pool, jnp.pad(o4.reshape(Bs, Q_W), ((0, Ms - Bs), (0, 0)))

        xs, ys, post_s = _mixer_half(xs, lw, attn_s, tabs_s, etab_s, tm_s, False)
        xp, xs = ffn(xp, xs)
        xp = ple(xp, lw['ple_norm'], lw['w_ple_gate'], pp[li], lw['w_ple_proj'], tm_p, 512)
        xs = ple(xs, lw['ple_norm'], lw['w_ple_gate'], ps[li], lw['w_ple_proj'], tm_s, 512)
        y, post = ys, post_s
        kc4, vc4, ks4, vs4, kw4, vw4 = (a[:Bs * KV_GROUPS].reshape(Bs, 1, KV_GROUPS, HEAD_DIM) for a in post[4:10])
        k_win = jnp.concatenate([state_k_win[li], kw4], axis=1)[:, -win_buf:]
        v_win = jnp.concatenate([state_v_win[li], vw4], axis=1)[:, -win_buf:]
        pool = jnp.concatenate([state_pool[li], y[:Bs, None, OFF_U:OFF_U + POOL_W]], axis=1)[:, -POOL_BUF:]
        states_s.append((kc4, vc4, ks4, vs4, k_win, v_win, pool))

    outs_p = [jnp.stack(a) for a in zip(*states_p)]
    outs_s = [jnp.stack(a) for a in zip(*states_s)]
    return (xp.reshape(B, T, D), xs[:Bs].reshape(Bs, 1, D), *outs_p, *outs_s)
```
